```python
import jax, jax.numpy as jnp
from jax import lax
import numpy as np

D_MODEL = 1024
BATCH = 8
SEQ = 4096
DEPTH = 2

MIX_WIDTH = 512
N_BRANCH = 4
MLA_HEADS = 8
QK_NOPE = 64
QK_ROPE = 32
V_HEAD = 64
Q_LORA = 384
KV_LORA = 256
ROPE_THETA = 10000.0
Q_BLOCK = 128
POOL_WINDOWS = (2, 4, 8, 16)
POOL_GROUP = MIX_WIDTH // 4
SSD_HEADS = 8
SSD_HEADDIM = 64
SSD_GROUPS = 2
SSD_STATE = 64
SSD_CHUNK = 128
CONV_WIDTH = 4
SSD_XBC = SSD_HEADS * SSD_HEADDIM + 2 * SSD_GROUPS * SSD_STATE
LRU_BLOCKS = 8
LRU_BLOCK = MIX_WIDTH // LRU_BLOCKS
LRU_C = 8.0
D_FF = 4 * D_MODEL
PLE_DIM = 256
EPS = 1e-6

SPLIT_SIZES = (Q_LORA, KV_LORA, QK_ROPE,
               MIX_WIDTH,
               MIX_WIDTH, SSD_XBC, SSD_HEADS,
               MIX_WIDTH, MIX_WIDTH,
               N_BRANCH * D_MODEL)
IN_COLS = sum(SPLIT_SIZES)

kernel_name = "hybrid_gated_mla_pool_ssd_rglru_block"


def _split_points():
    pts, acc = [], 0
    for s in SPLIT_SIZES[:-1]:
        acc += s
        pts.append(acc)
    return pts


def rmsnorm(x, g):
    x32 = x.astype(jnp.float32)
    y = x32 * lax.rsqrt(jnp.mean(x32 * x32, axis=-1, keepdims=True) + EPS)
    return (y * g.astype(jnp.float32)).astype(x.dtype)


def causal_dwconv(x, w, b):
    c = x.shape[-1]
    y = lax.conv_general_dilated(x, w[:, None, :].astype(x.dtype), window_strides=(1,),
                                 padding=[(CONV_WIDTH - 1, 0)],
                                 dimension_numbers=('NWC', 'WIO', 'NWC'),
                                 feature_group_count=c)
    return y + b.astype(x.dtype)


def rope_tables(positions):
    inv = 1.0 / (ROPE_THETA ** (jnp.arange(0, QK_ROPE, 2, dtype=jnp.float32) / QK_ROPE))
    ang = positions.astype(jnp.float32)[..., None] * inv
    return jnp.cos(ang), jnp.sin(ang)


def apply_rope(x, cos, sin):
    x32 = x.astype(jnp.float32)
    x1, x2 = jnp.split(x32, 2, axis=-1)
    out = jnp.concatenate([x1 * cos - x2 * sin, x2 * cos + x1 * sin], axis=-1)
    return out.astype(x.dtype)


def mla_mixer(c_q, c_kv, k_r, cos, sin, q_norm, w_uq, kv_norm, w_ukv):
    b, s, _ = c_q.shape
    q = (rmsnorm(c_q, q_norm) @ w_uq).reshape(b, s, MLA_HEADS, QK_NOPE + QK_ROPE)
    q_nope = q[..., :QK_NOPE]
    q_rope = apply_rope(q[..., QK_NOPE:], cos[:, :, None], sin[:, :, None])
    kv = (rmsnorm(c_kv, kv_norm) @ w_ukv).reshape(b, s, MLA_HEADS, QK_NOPE + V_HEAD)
    k_nope, v = kv[..., :QK_NOPE], kv[..., QK_NOPE:]
    k_rope = apply_rope(k_r, cos, sin)
    scale = (QK_NOPE + QK_ROPE) ** -0.5
    outs = []
    for blk in range(s // Q_BLOCK):
        q0, kend = blk * Q_BLOCK, (blk + 1) * Q_BLOCK
        sc = (jnp.einsum('bqhd,bkhd->bhqk', q_nope[:, q0:kend], k_nope[:, :kend])
              + jnp.einsum('bqhd,bkd->bhqk', q_rope[:, q0:kend], k_rope[:, :kend]))
        sc = sc.astype(jnp.float32) * scale
        qi = q0 + jnp.arange(Q_BLOCK)[:, None]
        ki = jnp.arange(kend)[None, :]
        sc = jnp.where(ki <= qi, sc, -jnp.inf)
        pr = jax.nn.softmax(sc, axis=-1).astype(v.dtype)
        outs.append(jnp.einsum('bhqk,bkhd->bqhd', pr, v[:, :kend]))
    o = jnp.concatenate(outs, axis=1)
    return o.reshape(b, s, MLA_HEADS * V_HEAD)


def pool_mixer(u, w_pool, pool_scale):
    b, s, _ = u.shape
    u32 = u.astype(jnp.float32)
    maxw = max(POOL_WINDOWS)
    cs = jnp.pad(jnp.cumsum(u32, axis=1), ((0, 0), (maxw, 0), (0, 0)))
    t = jnp.arange(s)
    groups = []
    for g, w in enumerate(POOL_WINDOWS):
        sl = slice(g * POOL_GROUP, (g + 1) * POOL_GROUP)
        win_sum = cs[:, maxw:, sl] - cs[:, maxw - w:maxw - w + s, sl]
        count = jnp.minimum(t + 1, w).astype(jnp.float32)[None, :, None]
        groups.append(win_sum / count - u32[..., sl])
    d = jnp.stack(groups, axis=2).astype(u.dtype)
    y = jnp.einsum('bsgc,gcd->bsgd', d, w_pool).reshape(b, s, MIX_WIDTH)
    return y * pool_scale


def segsum(a):
    t = a.shape[-1]
    cs = jnp.cumsum(a, axis=-1)
    diff = cs[..., :, None] - cs[..., None, :]
    mask = jnp.tril(jnp.ones((t, t), dtype=bool))
    return jnp.where(mask, diff, -jnp.inf)


def ssd_mixer(z, xbc, dt, conv_w, conv_b, dt_bias, a_log, d_skip, norm_g):
    b, s, _ = z.shape
    nc, lc, g, r, n, hp = s // SSD_CHUNK, SSD_CHUNK, SSD_GROUPS, SSD_HEADS // SSD_GROUPS, SSD_STATE, SSD_HEADDIM
    xbc = jax.nn.silu(causal_dwconv(xbc, conv_w, conv_b)).astype(jnp.float32)
    xs = xbc[..., :MIX_WIDTH]
    bm = xbc[..., MIX_WIDTH:MIX_WIDTH + g * n].reshape(b, nc, lc, g, n)
    cm = xbc[..., MIX_WIDTH + g * n:].reshape(b, nc, lc, g, n)
    dt = jax.nn.softplus(dt.astype(jnp.float32) + dt_bias.astype(jnp.float32))
    a_head = -jnp.exp(a_log.astype(jnp.float32))
    x = xs.reshape(b, nc, lc, g, r, hp)
    xdt = x * dt.reshape(b, nc, lc, g, r)[..., None]
    a = (dt * a_head).reshape(b, nc, lc, g, r).transpose(0, 3, 4, 1, 2)
    a_cs = jnp.cumsum(a, axis=-1)
    lmat = jnp.exp(segsum(a))
    cb = jnp.einsum('bclgn,bcsgn->bgcls', cm, bm)
    y_diag = jnp.einsum('bgrcls,bcsgrp->bclgrp', cb[:, :, None] * lmat, xdt)
    decay_states = jnp.exp(a_cs[..., -1:] - a_cs)
    states = jnp.einsum('bclgn,bgrcl,bclgrp->bcgrpn', bm, decay_states, xdt)
    states = jnp.concatenate([jnp.zeros_like(states[:, :1]), states], axis=1)
    chunk_a = jnp.pad(a_cs[..., -1], ((0, 0), (0, 0), (0, 0), (1, 0)))
    decay_chunk = jnp.exp(segsum(chunk_a))
    states = jnp.einsum('bgrzc,bcgrpn->bzgrpn', decay_chunk, states)[:, :-1]
    y_off = jnp.einsum('bclgn,bcgrpn,bgrcl->bclgrp', cm, states, jnp.exp(a_cs))
    y = (y_diag + y_off).reshape(b, s, SSD_HEADS, hp) \
        + xs.reshape(b, s, SSD_HEADS, hp) * d_skip.astype(jnp.float32)[:, None]
    y = y.reshape(b, s, MIX_WIDTH) * jax.nn.silu(z.astype(jnp.float32))
    return rmsnorm(y, norm_g).astype(z.dtype)


def rglru_mixer(gate_in, x_in, conv_w, conv_b, w_a, b_a, w_i, b_i, lam):
    b, s, _ = x_in.shape
    gate = jax.nn.gelu(gate_in)
    xc = causal_dwconv(x_in, conv_w, conv_b)
    xb = xc.reshape(b, s, LRU_BLOCKS, LRU_BLOCK)
    r_t = jax.nn.sigmoid((jnp.einsum('bshi,hij->bshj', xb, w_a).reshape(b, s, MIX_WIDTH) + b_a).astype(jnp.float32))
    i_t = jax.nn.sigmoid((jnp.einsum('bshi,hij->bshj', xb, w_i).reshape(b, s, MIX_WIDTH) + b_i).astype(jnp.float32))
    log_a = -LRU_C * r_t * jax.nn.softplus(-lam.astype(jnp.float32))
    a_t = jnp.exp(log_a)
    mult = jnp.sqrt(-jnp.expm1(2.0 * log_a))
    u = xc.astype(jnp.float32) * i_t * mult

    def combine(lhs, rhs):
        a1, b1 = lhs
        a2, b2 = rhs
        return a1 * a2, a2 * b1 + b2

    _, h = lax.associative_scan(combine, (a_t, u), axis=1)
    return h.astype(x_in.dtype) * gate


def setup_inputs(seed: int = 0) -> dict:
    key = jax.random.key(seed)
    ks = iter(jax.random.split(key, 48))
    f32 = jnp.float32

    def nrm(shape, fan_in):
        return jax.random.normal(next(ks), shape, f32) * (fan_in ** -0.5)

    def gain(shape):
        return 1.0 + 0.05 * jax.random.normal(next(ks), shape, f32)

    def small(shape):
        return 0.01 * jax.random.normal(next(ks), shape, f32)

    L = DEPTH
    x = jax.random.normal(next(ks), (BATCH, SEQ, D_MODEL), f32)
    p = jax.random.normal(next(ks), (DEPTH, BATCH, SEQ, PLE_DIM), f32)
    offs = jax.random.randint(next(ks), (BATCH, 1), 0, 1024, dtype=jnp.int32)
    positions = (offs + jnp.arange(SEQ, dtype=jnp.int32)[None, :]).astype(jnp.int32)
    dt0 = jnp.exp(jax.random.uniform(next(ks), (L, SSD_HEADS), f32, np.log(1e-3), np.log(1e-1)))
    dt_bias = dt0 + jnp.log(-jnp.expm1(-dt0))
    a_log = jnp.log(jax.random.uniform(next(ks), (L, SSD_HEADS), f32, 1.0, 16.0))
    a_pow = jax.random.uniform(next(ks), (L, MIX_WIDTH), f32, 0.9, 0.999) ** (1.0 / LRU_C)
    lam = jnp.log(a_pow) - jnp.log1p(-a_pow)
    return {
        "x": x,
        "p": p,
        "positions": positions,
        "g_mix": gain((L, D_MODEL)),
        "w_in": nrm((L, D_MODEL, IN_COLS), D_MODEL),
        "q_norm": gain((L, Q_LORA)),
        "w_uq": nrm((L, Q_LORA, MLA_HEADS * (QK_NOPE + QK_ROPE)), Q_LORA),
        "kv_norm": gain((L, KV_LORA)),
        "w_ukv": nrm((L, KV_LORA, MLA_HEADS * (QK_NOPE + V_HEAD)), KV_LORA),
        "w_pool": nrm((L, 4, POOL_GROUP, POOL_GROUP), POOL_GROUP),
        "pool_scale": 1.0 + 0.1 * jax.random.normal(next(ks), (L, MIX_WIDTH), f32),
        "ssd_conv_w": nrm((L, CONV_WIDTH, SSD_XBC), CONV_WIDTH),
        "ssd_conv_b": small((L, SSD_XBC)),
        "ssd_dt_bias": dt_bias,
        "ssd_a_log": a_log,
        "ssd_d": gain((L, SSD_HEADS)),
        "ssd_norm": gain((L, MIX_WIDTH)),
        "lru_conv_w": nrm((L, CONV_WIDTH, MIX_WIDTH), CONV_WIDTH),
        "lru_conv_b": small((L, MIX_WIDTH)),
        "lru_w_a": nrm((L, LRU_BLOCKS, LRU_BLOCK, LRU_BLOCK), LRU_BLOCK),
        "lru_b_a": small((L, MIX_WIDTH)),
        "lru_w_i": nrm((L, LRU_BLOCKS, LRU_BLOCK, LRU_BLOCK), LRU_BLOCK),
        "lru_b_i": small((L, MIX_WIDTH)),
        "lru_lambda": lam,
        "w_branch": nrm((L, N_BRANCH, MIX_WIDTH, D_MODEL), MIX_WIDTH),
        "w_out": nrm((L, D_MODEL, D_MODEL), D_MODEL),
        "g_mlp": gain((L, D_MODEL)),
        "w_ff1": nrm((L, D_MODEL, D_FF), D_MODEL),
        "w_ff2": nrm((L, D_FF, D_MODEL), D_FF),
        "g_ple": gain((L, D_MODEL)),
        "w_ple_gate": nrm((L, D_MODEL, D_MODEL), D_MODEL),
        "w_ple": nrm((L, PLE_DIM, D_MODEL), PLE_DIM),
        "g_final": gain((D_MODEL,)),
    }


def reference(x, p, positions, g_mix, w_in, q_norm, w_uq, kv_norm, w_ukv, w_pool, pool_scale,
              ssd_conv_w, ssd_conv_b, ssd_dt_bias, ssd_a_log, ssd_d, ssd_norm,
              lru_conv_w, lru_conv_b, lru_w_a, lru_b_a, lru_w_i, lru_b_i, lru_lambda,
              w_branch, w_out, g_mlp, w_ff1, w_ff2, g_ple, w_ple_gate, w_ple, g_final):
    b, s, _ = x.shape
    cos, sin = rope_tables(positions)
    pts = _split_points()
    for l in range(DEPTH):
        h = rmsnorm(x, g_mix[l])
        u = h @ w_in[l]
        c_q, c_kv, k_r, u_pool, z, xbc, dt, lru_g, lru_x, gates = jnp.split(u, pts, axis=-1)
        y_a = mla_mixer(c_q, c_kv, k_r, cos, sin, q_norm[l], w_uq[l], kv_norm[l], w_ukv[l])
        y_b = pool_mixer(u_pool, w_pool[l], pool_scale[l])
        y_c = ssd_mixer(z, xbc, dt, ssd_conv_w[l], ssd_conv_b[l], ssd_dt_bias[l], ssd_a_log[l],
                        ssd_d[l], ssd_norm[l])
        y_d = rglru_mixer(lru_g, lru_x, lru_conv_w[l], lru_conv_b[l], lru_w_a[l], lru_b_a[l],
                          lru_w_i[l], lru_b_i[l], lru_lambda[l])
        gates = jax.nn.sigmoid(gates.reshape(b, s, N_BRANCH, D_MODEL))
        merged = (gates[:, :, 0] * (y_a @ w_branch[l, 0])
                  + gates[:, :, 1] * (y_b @ w_branch[l, 1])
                  + gates[:, :, 2] * (y_c @ w_branch[l, 2])
                  + gates[:, :, 3] * (y_d @ w_branch[l, 3]))
        x = x + merged @ w_out[l]
        h2 = rmsnorm(x, g_mlp[l])
        x = x + jnp.square(jax.nn.relu(h2 @ w_ff1[l])) @ w_ff2[l]
        ple_gate = jax.nn.sigmoid(rmsnorm(x, g_ple[l]) @ w_ple_gate[l])
        x = x + (p[l] @ w_ple[l]) * ple_gate
    return rmsnorm(x, g_final)
```

```python
import functools

import jax
import jax.numpy as jnp
from jax import lax
from jax.experimental import pallas as pl
from jax.experimental.pallas import tpu as pltpu

F32 = jnp.float32
BF16 = jnp.bfloat16

D_MODEL = 1024
MIX = 512
N_BRANCH = 4
HEADS = 8
QK_NOPE = 64
QK_ROPE = 32
V_HEAD = 64
Q_LORA = 384
KV_LORA = 256
ROPE_THETA = 10000.0
POOL_WINDOWS = (2, 4, 8, 16)
POOL_GROUP = 128
POOL_HALO = 16
SSD_HEADS = 8
SSD_HEADDIM = 64
SSD_GROUPS = 2
SSD_STATE = 64
SSD_CHUNK = 128
CONV_WIDTH = 4
CONV_HALO = 8
SSD_XBC = 768
LRU_BLOCKS = 8
LRU_BLOCK = 64
LRU_C = 8.0
D_FF = 4096
PLE_DIM = 256
EPS = 1e-6
LANES = 128
SUBLANES = 8
HEAD_PAD = 128
NEG_BIG = -1e30
VMEM_LIMIT = 56 * 1024 * 1024

SEQ_TILE = 512
TOK_TILE = 512
ATT_TILE = 512


def _dot(a, b):
    return jnp.dot(a, b, preferred_element_type=F32)


def _dot_nt(a, b):
    return lax.dot_general(a, b, (((1,), (1,)), ((), ())), preferred_element_type=F32)


def _dot_tn(a, b):
    return lax.dot_general(a, b, (((0,), (0,)), ((), ())), preferred_element_type=F32)


def _split3(a):
    hi = a.astype(BF16)
    r1 = a - hi.astype(F32)
    mid = r1.astype(BF16)
    lo = (r1 - mid.astype(F32)).astype(BF16)
    return hi, mid, lo


def _sel_dot(a, sel):
    hi, mid, lo = _split3(a)
    return _dot(hi, sel) + _dot(mid, sel) + _dot(lo, sel)


def _rmsnorm(x, g):
    return x * lax.rsqrt(jnp.mean(x * x, axis=-1, keepdims=True) + EPS) * g


def _sigmoid(x):
    return jax.nn.sigmoid(x)


def _silu(x):
    return x * jax.nn.sigmoid(x)


def _softplus(x):
    return jnp.maximum(x, 0.0) + jnp.log1p(jnp.exp(-jnp.abs(x)))


def _gelu_tanh(x):
    return 0.5 * x * (1.0 + jnp.tanh(0.7978845608028654 * (x + 0.044715 * (x * x * x))))


def _params(*sem):
    return pltpu.CompilerParams(dimension_semantics=sem, vmem_limit_bytes=VMEM_LIMIT)


def _full(shape):
    n = len(shape)
    return pl.BlockSpec(shape, lambda *_: (0,) * n)


def _rope_kernel(ang_ref, cos_ref, sin_ref):
    a = ang_ref[...]
    cos_ref[...] = jnp.cos(a)
    sin_ref[...] = jnp.sin(a)


def _rope_tables(positions):
    b, s = positions.shape
    half = QK_ROPE // 2
    inv = 1.0 / (ROPE_THETA ** (jnp.arange(0, QK_ROPE, 2, dtype=F32) / QK_ROPE))
    ang = positions.astype(F32)[..., None] * inv
    rows = b * s * half // LANES
    blk = min(rows, 512)
    cos, sin = pl.pallas_call(
        _rope_kernel,
        grid=(rows // blk,),
        in_specs=[pl.BlockSpec((blk, LANES), lambda i: (i, 0))],
        out_specs=[pl.BlockSpec((blk, LANES), lambda i: (i, 0))] * 2,
        out_shape=[jax.ShapeDtypeStruct((rows, LANES), F32)] * 2,
        compiler_params=_params("arbitrary"),
        name="rope_tables",
    )(ang.reshape(rows, LANES))
    cos = cos.reshape(b, s, half)
    sin = sin.reshape(b, s, half)
    ones = jnp.ones((b, s, QK_NOPE), F32)
    zeros_n = jnp.zeros((b, s, QK_NOPE), F32)
    zeros_p = jnp.zeros((b, s, HEAD_PAD - QK_NOPE - QK_ROPE), F32)
    cosq = jnp.concatenate([ones, cos, cos, zeros_p], axis=-1)
    sinq = jnp.concatenate([zeros_n, -sin, sin, zeros_p], axis=-1)
    return cosq, sinq


MLA_IN_COLS = Q_LORA + KV_LORA + 2 * HEAD_PAD


def _mla_proj_kernel(x_ref, g_ref, win_ref, qn_ref, wq_ref, kvn_ref, wk_ref, wv_ref, cos_ref, sin_ref,
                     q_out, k_out, v_out):
    scale = (QK_NOPE + QK_ROPE) ** -0.5
    h = _rmsnorm(x_ref[0], g_ref[...]).astype(BF16)
    u = _dot(h, win_ref[...])
    c_q = u[:, :Q_LORA]
    c_kv = u[:, Q_LORA:Q_LORA + KV_LORA]
    kr = u[:, Q_LORA + KV_LORA:Q_LORA + KV_LORA + HEAD_PAD]
    kr_sw = u[:, Q_LORA + KV_LORA + HEAD_PAD:]
    cosq = cos_ref[0]
    sinq = sin_ref[0]
    cqn = _rmsnorm(c_q, qn_ref[...]).astype(BF16)
    qq = _dot(cqn, wq_ref[...])
    ckvn = _rmsnorm(c_kv, kvn_ref[...]).astype(BF16)
    kn = _dot(ckvn, wk_ref[...])
    v = _dot(ckvn, wv_ref[...])
    k_rope = kr * cosq + kr_sw * sinq
    wide = HEADS * HEAD_PAD
    for hd in range(HEADS):
        lo = hd * HEAD_PAD
        q_h = (qq[:, lo:lo + HEAD_PAD] * cosq + qq[:, wide + lo:wide + lo + HEAD_PAD] * sinq) * scale
        q_out[0, hd] = q_h.astype(BF16)
        k_out[0, hd] = (kn[:, lo:lo + HEAD_PAD] + k_rope).astype(BF16)
    for hp in range(HEADS // 2):
        v_out[0, hp] = v[:, hp * LANES:(hp + 1) * LANES].astype(BF16)


def _mla_weights(w_in_l, w_uq_l, w_ukv_l):
    half = QK_ROPE // 2
    pad_l, pad_r = QK_NOPE, HEAD_PAD - QK_NOPE - QK_ROPE
    o = Q_LORA + KV_LORA
    w_kr = w_in_l[:, o:o + QK_ROPE]
    w_kr_sw = jnp.concatenate([w_kr[:, half:], w_kr[:, :half]], axis=1)
    place = lambda w: jnp.pad(w, ((0, 0), (pad_l, pad_r)))
    win = jnp.concatenate([w_in_l[:, :o], place(w_kr), place(w_kr_sw)], axis=1).astype(BF16)
    wq3 = w_uq_l.reshape(Q_LORA, HEADS, QK_NOPE + QK_ROPE)
    wq_a = jnp.pad(wq3, ((0, 0), (0, 0), (0, pad_r))).reshape(Q_LORA, HEADS * HEAD_PAD)
    rope = wq3[..., QK_NOPE:]
    rope_sw = jnp.concatenate([rope[..., half:], rope[..., :half]], axis=-1)
    wq_b = jnp.pad(rope_sw, ((0, 0), (0, 0), (pad_l, pad_r))).reshape(Q_LORA, HEADS * HEAD_PAD)
    wq = jnp.concatenate([wq_a, wq_b], axis=1).astype(BF16)
    wkv3 = w_ukv_l.reshape(KV_LORA, HEADS, QK_NOPE + V_HEAD)
    wk = jnp.pad(wkv3[..., :QK_NOPE], ((0, 0), (0, 0), (0, HEAD_PAD - QK_NOPE))).reshape(KV_LORA, HEADS * HEAD_PAD)
    wv = wkv3[..., QK_NOPE:].reshape(KV_LORA, HEADS * V_HEAD)
    return win, wq, wk.astype(BF16), wv.astype(BF16)


def _mla_proj(x, g, win, qn, wq, kvn, wk, wv, cosq, sinq):
    b, s, d = x.shape
    ts = min(SEQ_TILE, s)
    tile = lambda w: pl.BlockSpec((1, ts, w), lambda i, j: (i, j, 0))
    head_out = lambda n: pl.BlockSpec((1, n, ts, HEAD_PAD), lambda i, j: (i, 0, j, 0))
    return pl.pallas_call(
        _mla_proj_kernel,
        grid=(b, s // ts),
        in_specs=[tile(d), _full((1, d)), _full(win.shape), _full((1, Q_LORA)), _full(wq.shape),
                  _full((1, KV_LORA)), _full(wk.shape), _full(wv.shape), tile(HEAD_PAD), tile(HEAD_PAD)],
        out_specs=[head_out(HEADS), head_out(HEADS), head_out(HEADS // 2)],
        out_shape=[jax.ShapeDtypeStruct((b, HEADS, s, HEAD_PAD), BF16),
                   jax.ShapeDtypeStruct((b, HEADS, s, HEAD_PAD), BF16),
                   jax.ShapeDtypeStruct((b, HEADS // 2, s, LANES), BF16)],
        compiler_params=_params("arbitrary", "arbitrary"),
        name="mla_proj",
    )(x, g.reshape(1, d), win, qn.reshape(1, -1), wq, kvn.reshape(1, -1), wk, wv, cosq, sinq)


def _attn_kernel(q_ref, k_ref, v_ref, o_ref, *, tq):
    qi = pl.program_id(2)
    row = lax.broadcasted_iota(jnp.int32, (tq, tq), 0)
    col = lax.broadcasted_iota(jnp.int32, (tq, tq), 1)
    causal = row >= col
    outs = []
    for hh in range(2):
        q = q_ref[0, hh]

        def step(kb, carry, masked, hh=hh, q=q):
            m, l, acc = carry
            k0 = pl.multiple_of(kb * tq, tq)
            k = k_ref[0, hh, pl.ds(k0, tq), :]
            v = v_ref[0, 0, pl.ds(k0, tq), :]
            sc = _dot_nt(q, k)
            if masked:
                sc = jnp.where(causal, sc, NEG_BIG)
            m_new = jnp.maximum(m, jnp.max(sc, axis=-1, keepdims=True))
            alpha = jnp.exp(m - m_new)
            p = jnp.exp(sc - m_new)
            l = alpha * l + jnp.sum(p, axis=-1, keepdims=True)
            acc = alpha * acc + _dot(p.astype(BF16), v)
            return m_new, l, acc

        init = (jnp.full((tq, 1), NEG_BIG, F32), jnp.zeros((tq, 1), F32), jnp.zeros((tq, LANES), F32))
        carry = lax.fori_loop(0, qi, functools.partial(step, masked=False), init)
        m, l, acc = step(qi, carry, True)
        outs.append(acc / l)
    lane = lax.broadcasted_iota(jnp.int32, (tq, LANES), 1)
    o_ref[0] = jnp.where(lane < V_HEAD, outs[0], outs[1]).astype(BF16)


def _attention(q, k, v):
    b, _, s, _ = q.shape
    tq = min(ATT_TILE, s)
    return pl.pallas_call(
        functools.partial(_attn_kernel, tq=tq),
        grid=(b, HEADS // 2, s // tq),
        in_specs=[pl.BlockSpec((1, 2, tq, HEAD_PAD), lambda i, hp, j: (i, hp, j, 0)),
                  pl.BlockSpec((1, 2, s, HEAD_PAD), lambda i, hp, j: (i, hp, 0, 0)),
                  pl.BlockSpec((1, 1, s, LANES), lambda i, hp, j: (i, hp, 0, 0))],
        out_specs=pl.BlockSpec((1, tq, LANES), lambda i, hp, j: (i, j, hp)),
        out_shape=jax.ShapeDtypeStruct((b, s, MIX), BF16),
        compiler_params=_params("arbitrary", "arbitrary", "arbitrary"),
        name="mla_attention",
    )(q, k, v)


def _pool_kernel(x_ref, g_ref, win_ref, wp_ref, ps_ref, o_ref, ext_ref, *, ts):
    si = pl.program_id(1)

    @pl.when(si == 0)
    def _():
        ext_ref[0:POOL_HALO, :] = jnp.zeros((POOL_HALO, MIX), F32)

    h = _rmsnorm(x_ref[0], g_ref[...]).astype(BF16)
    u = _dot(h, win_ref[...])
    ext_ref[POOL_HALO:POOL_HALO + ts, :] = u
    t = si * ts + lax.broadcasted_iota(jnp.int32, (ts, 1), 0)
    for g, w in enumerate(POOL_WINDOWS):
        lo = g * POOL_GROUP
        u_g = u[:, lo:lo + POOL_GROUP]
        win = u_g
        for kk in range(1, w):
            win = win + ext_ref[POOL_HALO - kk:POOL_HALO - kk + ts, lo:lo + POOL_GROUP]
        count = jnp.minimum(t + 1, w).astype(F32)
        d = win / count - u_g
        y = _dot(d.astype(BF16), wp_ref[g])
        o_ref[0, :, lo:lo + POOL_GROUP] = (y * ps_ref[:, lo:lo + POOL_GROUP]).astype(BF16)
    ext_ref[0:POOL_HALO, :] = ext_ref[ts:ts + POOL_HALO, :]


def _pool(x, g, win, wp, ps):
    b, s, d = x.shape
    ts = min(SEQ_TILE, s)
    return pl.pallas_call(
        functools.partial(_pool_kernel, ts=ts),
        grid=(b, s // ts),
        in_specs=[pl.BlockSpec((1, ts, d), lambda i, j: (i, j, 0)), _full((1, d)), _full(win.shape),
                  _full(wp.shape), _full((1, MIX))],
        out_specs=pl.BlockSpec((1, ts, MIX), lambda i, j: (i, j, 0)),
        out_shape=jax.ShapeDtypeStruct((b, s, MIX), BF16),
        scratch_shapes=[pltpu.VMEM((POOL_HALO + ts, MIX), F32)],
        compiler_params=_params("arbitrary", "arbitrary"),
        name="pool_mixer",
    )(x, g.reshape(1, d), win, wp, ps.reshape(1, MIX))


def _causal_conv(ext_ref, u, cw_ref, cb_ref, ts, first):
    width = ext_ref.shape[1]

    @pl.when(first)
    def _():
        ext_ref[0:CONV_HALO, :] = jnp.zeros((CONV_HALO, width), F32)

    ext_ref[CONV_HALO:CONV_HALO + ts, :] = u
    acc = u * cw_ref[CONV_WIDTH - 1:CONV_WIDTH, :] + cb_ref[...]
    for j in range(CONV_WIDTH - 1):
        back = CONV_WIDTH - 1 - j
        acc = acc + ext_ref[CONV_HALO - back:CONV_HALO - back + ts, :] * cw_ref[j:j + 1, :]
    ext_ref[0:CONV_HALO, :] = ext_ref[ts:ts + CONV_HALO, :]
    return acc


SSD_IN_COLS = MIX + SSD_XBC + LANES
SSD_GROUP_COLS = MIX // SSD_GROUPS
SSD_GROUP_HEADS = SSD_HEADS // SSD_GROUPS


def _ssd_kernel(x_ref, g_ref, win_ref, cw_ref, cb_ref, dtb_ref, alog_ref, dskip_ref, ng_ref,
                tri_ref, expand_ref, bd_ref, o_ref, ext_ref, xbc_ref, state_ref, *, ts):
    si = pl.program_id(1)

    @pl.when(si == 0)
    def _():
        state_ref[...] = jnp.zeros(state_ref.shape, F32)

    h = _rmsnorm(x_ref[0], g_ref[...]).astype(BF16)
    u = _dot(h, win_ref[...])
    xbc_ref[...] = _silu(_causal_conv(ext_ref, u[:, MIX:MIX + SSD_XBC], cw_ref, cb_ref, ts, si == 0))
    z = u[:, :MIX]
    dt_all = _softplus(u[:, MIX + SSD_XBC:] + dtb_ref[...])
    a_all = dt_all * (-jnp.exp(alog_ref[...]))
    lc = SSD_CHUNK
    row = lax.broadcasted_iota(jnp.int32, (lc, lc), 0)
    col = lax.broadcasted_iota(jnp.int32, (lc, lc), 1)
    tril = row >= col
    lane = lax.broadcasted_iota(jnp.int32, (1, LANES), 1)
    ys = []
    for c in range(ts // lc):
        r0 = c * lc
        xs = xbc_ref[r0:r0 + lc, 0:MIX]
        bm = xbc_ref[r0:r0 + lc, MIX:MIX + LANES]
        cm = xbc_ref[r0:r0 + lc, MIX + LANES:MIX + 2 * LANES]
        dt = dt_all[r0:r0 + lc]
        cs = _sel_dot_left(tri_ref[...], a_all[r0:r0 + lc])
        cs_t = cs.T
        cs_end = cs[lc - 1:lc, :]
        per_head = jnp.concatenate([dt, jnp.exp(cs), dt * jnp.exp(cs_end - cs)], axis=0)
        wide = _sel_dot(per_head, expand_ref[...])
        dt_x, ecs_x, dd_x = wide[0:lc], wide[lc:2 * lc], wide[2 * lc:3 * lc]
        xdt = (xs * dt_x).astype(BF16)
        xdd = (xs * dd_x).astype(BF16)
        y_parts = []
        for grp in range(SSD_GROUPS):
            in_grp = (lane >= grp * SSD_STATE) & (lane < (grp + 1) * SSD_STATE)
            cm_g = jnp.where(in_grp, cm, 0.0).astype(BF16)
            bm_g = jnp.where(in_grp, bm, 0.0).astype(BF16)
            cb = _dot_nt(cm_g, bm_g)
            gc = grp * SSD_GROUP_COLS
            masks = []
            for hl in range(SSD_GROUP_HEADS):
                hd = grp * SSD_GROUP_HEADS + hl
                seg = jnp.where(tril, cs[:, hd:hd + 1] - cs_t[hd:hd + 1, :], NEG_BIG)
                masks.append((cb * jnp.exp(seg)).astype(BF16))
            m_cat = jnp.concatenate(masks, axis=1)
            x_bd = jnp.tile(xdt[:, gc:gc + SSD_GROUP_COLS], (SSD_GROUP_HEADS, 1)) * bd_ref[...]
            y_diag = _dot(m_cat, x_bd)
            st = state_ref[grp]
            y_off = _dot(cm_g, st.astype(BF16)) * ecs_x[:, gc:gc + SSD_GROUP_COLS]
            state_ref[grp] = (st * ecs_x[lc - 1:lc, gc:gc + SSD_GROUP_COLS]
                              + _dot_tn(bm_g, xdd[:, gc:gc + SSD_GROUP_COLS]))
            y_parts.append(y_diag + y_off)
        ys.append(jnp.concatenate(y_parts, axis=1) + xs * dskip_ref[...])
    y = jnp.concatenate(ys, axis=0) * _silu(z)
    o_ref[0] = _rmsnorm(y, ng_ref[...]).astype(BF16)


def _sel_dot_left(sel, a):
    hi, mid, lo = _split3(a)
    return _dot(sel, hi) + _dot(sel, mid) + _dot(sel, lo)


def _ssd_constants():
    lc = SSD_CHUNK
    tri = (jnp.arange(lc)[:, None] >= jnp.arange(lc)[None, :]).astype(BF16)
    expand = (jnp.arange(LANES)[:, None] == (jnp.arange(MIX)[None, :] // SSD_HEADDIM)).astype(BF16)
    rows = jnp.arange(SSD_GROUP_HEADS * lc)[:, None] // lc
    cols = jnp.arange(SSD_GROUP_COLS)[None, :] // SSD_HEADDIM
    bd = (rows == cols).astype(BF16)
    return tri, expand, bd


def _ssd(x, g, win, cw, cb, dtb, alog, dskip, ng):
    b, s, d = x.shape
    ts = min(SEQ_TILE, s)
    tri, expand, bd = _ssd_constants()
    pad = lambda v: jnp.pad(v, (0, LANES - SSD_HEADS)).reshape(1, LANES)
    return pl.pallas_call(
        functools.partial(_ssd_kernel, ts=ts),
        grid=(b, s // ts),
        in_specs=[pl.BlockSpec((1, ts, d), lambda i, j: (i, j, 0)), _full((1, d)), _full(win.shape),
                  _full((CONV_WIDTH, SSD_XBC)), _full((1, SSD_XBC)), _full((1, LANES)), _full((1, LANES)),
                  _full((1, MIX)), _full((1, MIX)), _full(tri.shape), _full(expand.shape), _full(bd.shape)],
        out_specs=pl.BlockSpec((1, ts, MIX), lambda i, j: (i, j, 0)),
        out_shape=jax.ShapeDtypeStruct((b, s, MIX), BF16),
        scratch_shapes=[pltpu.VMEM((CONV_HALO + ts, SSD_XBC), F32),
                        pltpu.VMEM((ts, SSD_XBC), F32),
                        pltpu.VMEM((SSD_GROUPS, LANES, SSD_GROUP_COLS), F32)],
        compiler_params=_params("arbitrary", "arbitrary"),
        name="ssd_mixer",
    )(x, g.reshape(1, d), win, cw, cb.reshape(1, -1), pad(dtb), pad(alog),
      jnp.repeat(dskip, SSD_HEADDIM).reshape(1, MIX), ng.reshape(1, MIX), tri, expand, bd)


def _lru_kernel(x_ref, g_ref, win_ref, cw_ref, cb_ref, wai_ref, bai_ref, lam_ref, o_ref,
                ext_ref, a_ref, h_ref, carry_ref, *, ts):
    si = pl.program_id(1)

    @pl.when(si == 0)
    def _():
        carry_ref[...] = jnp.zeros(carry_ref.shape, F32)

    hn = _rmsnorm(x_ref[0], g_ref[...]).astype(BF16)
    u = _dot(hn, win_ref[...])
    xc = _causal_conv(ext_ref, u[:, MIX:], cw_ref, cb_ref, ts, si == 0)
    ri = _dot(xc.astype(BF16), wai_ref[...]) + bai_ref[...]
    r_t = _sigmoid(ri[:, :MIX])
    i_t = _sigmoid(ri[:, MIX:])
    log_a = (-LRU_C) * r_t * _softplus(-lam_ref[...])
    a_t = jnp.exp(log_a)
    mult = jnp.sqrt(jnp.tanh(-log_a) * (a_t * a_t + 1.0))
    a_ref[...] = a_t
    h_ref[...] = xc * i_t * mult
    sub = lax.broadcasted_iota(jnp.int32, (SUBLANES, MIX), 0)

    def block(j, carry):
        r0 = pl.multiple_of(j * SUBLANES, SUBLANES)
        a8 = a_ref[pl.ds(r0, SUBLANES), :]
        u8 = h_ref[pl.ds(r0, SUBLANES), :]
        for kk in (1, 2, 4):
            keep = sub >= kk
            u8 = jnp.where(keep, a8 * pltpu.roll(u8, kk, 0) + u8, u8)
            a8 = jnp.where(keep, a8 * pltpu.roll(a8, kk, 0), a8)
        h8 = u8 + a8 * carry
        h_ref[pl.ds(r0, SUBLANES), :] = h8
        return h8[SUBLANES - 1:SUBLANES, :]

    carry_ref[...] = lax.fori_loop(0, ts // SUBLANES, block, carry_ref[...])
    o_ref[0] = (h_ref[...] * _gelu_tanh(u[:, :MIX])).astype(BF16)


def _block_diag(w):
    nb, n, _ = w.shape
    eye = jnp.eye(nb, dtype=w.dtype)
    return (eye[:, None, :, None] * w[:, :, None, :]).reshape(nb * n, nb * n)


def _lru(x, g, win, cw, cb, wai, bai, lam):
    b, s, d = x.shape
    ts = min(SEQ_TILE, s)
    return pl.pallas_call(
        functools.partial(_lru_kernel, ts=ts),
        grid=(b, s // ts),
        in_specs=[pl.BlockSpec((1, ts, d), lambda i, j: (i, j, 0)), _full((1, d)), _full(win.shape),
                  _full((CONV_WIDTH, MIX)), _full((1, MIX)), _full(wai.shape), _full((1, 2 * MIX)), _full((1, MIX))],
        out_specs=pl.BlockSpec((1, ts, MIX), lambda i, j: (i, j, 0)),
        out_shape=jax.ShapeDtypeStruct((b, s, MIX), BF16),
        scratch_shapes=[pltpu.VMEM((CONV_HALO + ts, MIX), F32), pltpu.VMEM((ts, MIX), F32),
                        pltpu.VMEM((ts, MIX), F32), pltpu.VMEM((1, MIX), F32)],
        compiler_params=_params("arbitrary", "arbitrary"),
        name="rglru_mixer",
    )(x, g.reshape(1, d), win, cw, cb.reshape(1, MIX), wai, bai.reshape(1, 2 * MIX), lam.reshape(1, MIX))


def _merge_kernel(x_ref, g_ref, wg_ref, ya_ref, yb_ref, yc_ref, yd_ref, wb_ref, wo_ref, o_ref):
    x = x_ref[...]
    h = _rmsnorm(x, g_ref[...]).astype(BF16)
    merged = None
    for n, y_ref in enumerate((ya_ref, yb_ref, yc_ref, yd_ref)):
        gate = _sigmoid(_dot(h, wg_ref[:, n * D_MODEL:(n + 1) * D_MODEL]))
        term = gate * _dot(y_ref[...], wb_ref[n])
        merged = term if merged is None else merged + term
    o_ref[...] = x + _dot(merged.astype(BF16), wo_ref[...])


def _merge(x2, g, wg, ys, wb, wo):
    t, d = x2.shape
    tm = min(TOK_TILE, t)
    row = lambda w: pl.BlockSpec((tm, w), lambda i: (i, 0))
    return pl.pallas_call(
        _merge_kernel,
        grid=(t // tm,),
        in_specs=[row(d), _full((1, d)), _full(wg.shape), row(MIX), row(MIX), row(MIX), row(MIX),
                  _full(wb.shape), _full(wo.shape)],
        out_specs=row(d),
        out_shape=jax.ShapeDtypeStruct((t, d), F32),
        compiler_params=_params("arbitrary"),
        name="gated_merge",
    )(x2, g.reshape(1, d), wg, *ys, wb, wo)


FF_CHUNK = 1024


def _ffn_kernel(x_ref, g_ref, w1_ref, w2_ref, o_ref):
    x = x_ref[...]
    h = _rmsnorm(x, g_ref[...]).astype(BF16)
    acc = x
    for c in range(D_FF // FF_CHUNK):
        hid = jnp.maximum(_dot(h, w1_ref[:, c * FF_CHUNK:(c + 1) * FF_CHUNK]), 0.0)
        acc = acc + _dot((hid * hid).astype(BF16), w2_ref[c * FF_CHUNK:(c + 1) * FF_CHUNK, :])
    o_ref[...] = acc


def _ffn(x2, g, w1, w2):
    t, d = x2.shape
    tm = min(TOK_TILE, t)
    row = pl.BlockSpec((tm, d), lambda i: (i, 0))
    return pl.pallas_call(
        _ffn_kernel,
        grid=(t // tm,),
        in_specs=[row, _full((1, d)), _full(w1.shape), _full(w2.shape)],
        out_specs=row,
        out_shape=jax.ShapeDtypeStruct((t, d), F32),
        compiler_params=_params("arbitrary"),
        name="sqrelu_mlp",
    )(x2, g.reshape(1, d), w1, w2)


def _ple_kernel(x_ref, g_ref, wpg_ref, p_ref, wple_ref, gf_ref, o_ref, *, final_norm):
    x = x_ref[...]
    gate = _sigmoid(_dot(_rmsnorm(x, g_ref[...]).astype(BF16), wpg_ref[...]))
    y = x + _dot(p_ref[...].astype(BF16), wple_ref[...]) * gate
    if final_norm:
        y = _rmsnorm(y, gf_ref[...])
    o_ref[...] = y


def _ple(x2, g, wpg, p2, wple, g_final, final_norm):
    t, d = x2.shape
    tm = min(TOK_TILE, t)
    row = lambda w: pl.BlockSpec((tm, w), lambda i: (i, 0))
    return pl.pallas_call(
        functools.partial(_ple_kernel, final_norm=final_norm),
        grid=(t // tm,),
        in_specs=[row(d), _full((1, d)), _full(wpg.shape), row(PLE_DIM), _full(wple.shape), _full((1, d))],
        out_specs=row(d),
        out_shape=jax.ShapeDtypeStruct((t, d), F32),
        compiler_params=_params("arbitrary"),
        name="ple_gate",
    )(x2, g.reshape(1, d), wpg, p2, wple, g_final.reshape(1, d))


_SPLIT = (Q_LORA, KV_LORA, QK_ROPE, MIX, MIX, SSD_XBC, SSD_HEADS, MIX, MIX, N_BRANCH * D_MODEL)


def _col_offsets():
    offs, acc = [], 0
    for sz in _SPLIT:
        offs.append(acc)
        acc += sz
    return offs


def kernel(x, p, positions, g_mix, w_in, q_norm, w_uq, kv_norm, w_ukv, w_pool, pool_scale,
           ssd_conv_w, ssd_conv_b, ssd_dt_bias, ssd_a_log, ssd_d, ssd_norm,
           lru_conv_w, lru_conv_b, lru_w_a, lru_b_a, lru_w_i, lru_b_i, lru_lambda,
           w_branch, w_out, g_mlp, w_ff1, w_ff2, g_ple, w_ple_gate, w_ple, g_final):
    b, s, d = x.shape
    depth = w_in.shape[0]
    t = b * s
    offs = _col_offsets()
    o_pool, o_z, o_xbc, o_dt, o_lg, o_lx, o_gate = offs[3], offs[4], offs[5], offs[6], offs[7], offs[8], offs[9]
    cosq, sinq = _rope_tables(positions)
    for l in range(depth):
        wl = w_in[l]
        g = g_mix[l]
        win_a, wq, wk, wv = _mla_weights(wl, w_uq[l], w_ukv[l])
        q, k, v = _mla_proj(x, g, win_a, q_norm[l], wq, kv_norm[l], wk, wv, cosq, sinq)
        y_a = _attention(q, k, v)
        y_b = _pool(x, g, wl[:, o_pool:o_pool + MIX].astype(BF16), w_pool[l].astype(BF16), pool_scale[l])
        w_dt = jnp.pad(wl[:, o_dt:o_dt + SSD_HEADS], ((0, 0), (0, LANES - SSD_HEADS)))
        win_c = jnp.concatenate([wl[:, o_z:o_z + MIX], wl[:, o_xbc:o_xbc + SSD_XBC], w_dt], axis=1).astype(BF16)
        y_c = _ssd(x, g, win_c, ssd_conv_w[l], ssd_conv_b[l], ssd_dt_bias[l], ssd_a_log[l], ssd_d[l], ssd_norm[l])
        win_d = wl[:, o_lg:o_lg + 2 * MIX].astype(BF16)
        wai = jnp.concatenate([_block_diag(lru_w_a[l]), _block_diag(lru_w_i[l])], axis=1).astype(BF16)
        bai = jnp.concatenate([lru_b_a[l], lru_b_i[l]])
        y_d = _lru(x, g, win_d, lru_conv_w[l], lru_conv_b[l], wai, bai, lru_lambda[l])
        ys = [y.reshape(t, MIX) for y in (y_a, y_b, y_c, y_d)]
        x2 = _merge(x.reshape(t, d), g, wl[:, o_gate:].astype(BF16), ys, w_branch[l].astype(BF16),
                    w_out[l].astype(BF16))
        x2 = _ffn(x2, g_mlp[l], w_ff1[l].astype(BF16), w_ff2[l].astype(BF16))
        x2 = _ple(x2, g_ple[l], w_ple_gate[l].astype(BF16), p[l].reshape(t, PLE_DIM), w_ple[l].astype(BF16),
                  g_final, final_norm=(l == depth - 1))
        x = x2.reshape(b, s, d)
    return x
```

```python
import functools

import jax
import jax.numpy as jnp
import numpy as np
from jax import lax
from jax.experimental import pallas as pl
from jax.experimental.pallas import tpu as pltpu

F32 = jnp.float32
BF16 = jnp.bfloat16

D_MODEL = 1024
MIX = 512
N_BRANCH = 4
HEADS = 8
QK_NOPE = 64
QK_ROPE = 32
V_HEAD = 64
Q_LORA = 384
KV_LORA = 256
ROPE_THETA = 10000.0
POOL_WINDOWS = (2, 4, 8, 16)
POOL_GROUP = 128
POOL_HALO = 16
SSD_HEADS = 8
SSD_HEADDIM = 64
SSD_GROUPS = 2
SSD_STATE = 64
SSD_CHUNK = 128
CONV_WIDTH = 4
CONV_HALO = 8
SSD_XBC = 768
LRU_BLOCKS = 8
LRU_BLOCK = 64
LRU_C = 8.0
D_FF = 4096
PLE_DIM = 256
EPS = 1e-6
LANES = 128
SUBLANES = 8
HEAD_PAD = 128
NEG_BIG = -1e30
LOG2_E = 1.4426950408889634
VMEM_LIMIT = 56 * 1024 * 1024

SEQ_TILE = 512
TOK_TILE = 512
ATT_TILE = 512
ATT_HEADS = 4


def _dot(a, b):
    return jnp.dot(a, b, preferred_element_type=F32)


def _dot_nt(a, b):
    return lax.dot_general(a, b, (((1,), (1,)), ((), ())), preferred_element_type=F32)


def _dot_tn(a, b):
    return lax.dot_general(a, b, (((0,), (0,)), ((), ())), preferred_element_type=F32)


def _split3(a):
    hi = a.astype(BF16)
    r1 = a - hi.astype(F32)
    mid = r1.astype(BF16)
    lo = (r1 - mid.astype(F32)).astype(BF16)
    return hi, mid, lo


def _sel_dot(a, sel):
    hi, mid, lo = _split3(a)
    return _dot(hi, sel) + _dot(mid, sel) + _dot(lo, sel)


def _rmsnorm(x, g):
    return x * lax.rsqrt(jnp.mean(x * x, axis=-1, keepdims=True) + EPS) * g


def _sigmoid(x):
    return jax.nn.sigmoid(x)


def _silu(x):
    return x * jax.nn.sigmoid(x)


def _softplus(x):
    return jnp.maximum(x, 0.0) + jnp.log1p(jnp.exp(-jnp.abs(x)))


def _gelu_tanh(x):
    return 0.5 * x * (1.0 + jnp.tanh(0.7978845608028654 * (x + 0.044715 * (x * x * x))))


def _params(*sem):
    return pltpu.CompilerParams(dimension_semantics=sem, vmem_limit_bytes=VMEM_LIMIT)


def _full(shape):
    n = len(shape)
    return pl.BlockSpec(shape, lambda *_: (0,) * n)


def _resident(shape):
    n = len(shape)
    return pl.BlockSpec(shape, lambda *_: (0,) * n, pipeline_mode=pl.Buffered(1))


ROPE_HALF = QK_ROPE // 2
ROPE_PACK = LANES // ROPE_HALF


def _rope_kernel(ang_ref, place_ref, cos_ref, sin_ref, *, blk):
    a = ang_ref[...]
    cos_parts = _split3(jnp.cos(a))
    sin_parts = _split3(jnp.sin(a))
    lane = lax.broadcasted_iota(jnp.int32, (1, LANES), 1)
    ones = (lane < QK_NOPE).astype(F32)
    for i in range(ROPE_PACK):
        ci = sum(_dot(part, place_ref[0, i]) for part in cos_parts) + ones
        si = sum(_dot(part, place_ref[1, i]) for part in sin_parts)
        cos_ref[pl.ds(i, blk, stride=ROPE_PACK), :] = ci
        sin_ref[pl.ds(i, blk, stride=ROPE_PACK), :] = si


def _rope_placement():
    place = np.zeros((2, ROPE_PACK, LANES, LANES), np.float32)
    for i in range(ROPE_PACK):
        for j in range(ROPE_HALF):
            src = i * ROPE_HALF + j
            place[0, i, src, QK_NOPE + j] = 1.0
            place[0, i, src, QK_NOPE + ROPE_HALF + j] = 1.0
            place[1, i, src, QK_NOPE + j] = -1.0
            place[1, i, src, QK_NOPE + ROPE_HALF + j] = 1.0
    return jnp.asarray(place, BF16)


def _rope_tables(positions):
    b, s = positions.shape
    t = b * s
    inv = 1.0 / (ROPE_THETA ** (jnp.arange(0, QK_ROPE, 2, dtype=F32) / QK_ROPE))
    rows = t // ROPE_PACK
    pos = positions.astype(F32).reshape(rows, ROPE_PACK)
    ang = jnp.repeat(pos, ROPE_HALF, axis=1) * jnp.tile(inv, ROPE_PACK)[None, :]
    blk = min(rows, 512)
    place = _rope_placement()
    out_spec = pl.BlockSpec((blk * ROPE_PACK, LANES), lambda i: (i, 0))
    cosq, sinq = pl.pallas_call(
        functools.partial(_rope_kernel, blk=blk),
        grid=(rows // blk,),
        in_specs=[pl.BlockSpec((blk, LANES), lambda i: (i, 0)), _full(place.shape)],
        out_specs=[out_spec, out_spec],
        out_shape=[jax.ShapeDtypeStruct((t, LANES), F32)] * 2,
        compiler_params=_params("arbitrary"),
        name="rope_tables",
    )(ang, place)
    return cosq.reshape(b, s, LANES), sinq.reshape(b, s, LANES)


MLA_IN_COLS = Q_LORA + KV_LORA + 2 * HEAD_PAD


def _mla_proj_kernel(x_ref, g_ref, win_ref, qn_ref, wq_ref, kvn_ref, wk_ref, wv_ref, cos_ref, sin_ref,
                     q_out, k_out, v_out):
    scale = (QK_NOPE + QK_ROPE) ** -0.5 * LOG2_E
    h = _rmsnorm(x_ref[0], g_ref[...]).astype(BF16)
    u = _dot(h, win_ref[...])
    c_q = u[:, :Q_LORA]
    c_kv = u[:, Q_LORA:Q_LORA + KV_LORA]
    kr = u[:, Q_LORA + KV_LORA:Q_LORA + KV_LORA + HEAD_PAD]
    kr_sw = u[:, Q_LORA + KV_LORA + HEAD_PAD:]
    cosq = cos_ref[0]
    sinq = sin_ref[0]
    cqn = _rmsnorm(c_q, qn_ref[...]).astype(BF16)
    qq = _dot(cqn, wq_ref[...])
    ckvn = _rmsnorm(c_kv, kvn_ref[...]).astype(BF16)
    kn = _dot(ckvn, wk_ref[...])
    v = _dot(ckvn, wv_ref[...])
    lane = lax.broadcasted_iota(jnp.int32, (1, HEAD_PAD), 1)
    ones_lane = (lane == V_HEAD).astype(F32)
    k_rope = kr * cosq + kr_sw * sinq
    wide = HEADS * HEAD_PAD
    for hd in range(HEADS):
        lo = hd * HEAD_PAD
        q_h = (qq[:, lo:lo + HEAD_PAD] * cosq + qq[:, wide + lo:wide + lo + HEAD_PAD] * sinq) * scale
        q_out[0, hd] = q_h.astype(BF16)
        k_out[0, hd] = (kn[:, lo:lo + HEAD_PAD] + k_rope).astype(BF16)
        v_out[0, hd] = (v[:, lo:lo + HEAD_PAD] + ones_lane).astype(BF16)


def _mla_weights(w_in_l, w_uq_l, w_ukv_l):
    half = QK_ROPE // 2
    pad_l, pad_r = QK_NOPE, HEAD_PAD - QK_NOPE - QK_ROPE
    o = Q_LORA + KV_LORA
    w_kr = w_in_l[:, o:o + QK_ROPE]
    w_kr_sw = jnp.concatenate([w_kr[:, half:], w_kr[:, :half]], axis=1)
    place = lambda w: jnp.pad(w, ((0, 0), (pad_l, pad_r)))
    win = jnp.concatenate([w_in_l[:, :o], place(w_kr), place(w_kr_sw)], axis=1).astype(BF16)
    wq3 = w_uq_l.reshape(Q_LORA, HEADS, QK_NOPE + QK_ROPE)
    wq_a = jnp.pad(wq3, ((0, 0), (0, 0), (0, pad_r))).reshape(Q_LORA, HEADS * HEAD_PAD)
    rope = wq3[..., QK_NOPE:]
    rope_sw = jnp.concatenate([rope[..., half:], rope[..., :half]], axis=-1)
    wq_b = jnp.pad(rope_sw, ((0, 0), (0, 0), (pad_l, pad_r))).reshape(Q_LORA, HEADS * HEAD_PAD)
    wq = jnp.concatenate([wq_a, wq_b], axis=1).astype(BF16)
    wkv3 = w_ukv_l.reshape(KV_LORA, HEADS, QK_NOPE + V_HEAD)
    wk = jnp.pad(wkv3[..., :QK_NOPE], ((0, 0), (0, 0), (0, HEAD_PAD - QK_NOPE))).reshape(KV_LORA, HEADS * HEAD_PAD)
    wv = jnp.pad(wkv3[..., QK_NOPE:], ((0, 0), (0, 0), (0, HEAD_PAD - V_HEAD))).reshape(KV_LORA, HEADS * HEAD_PAD)
    return win, wq, wk.astype(BF16), wv.astype(BF16)


def _mla_proj(x, g, win, qn, wq, kvn, wk, wv, cosq, sinq):
    b, s, d = x.shape
    ts = min(SEQ_TILE, s)
    tile = lambda w: pl.BlockSpec((1, ts, w), lambda i, j: (i, j, 0))
    head_out = lambda n: pl.BlockSpec((1, n, ts, HEAD_PAD), lambda i, j: (i, 0, j, 0))
    return pl.pallas_call(
        _mla_proj_kernel,
        grid=(b, s // ts),
        in_specs=[tile(d), _full((1, d)), _full(win.shape), _full((1, Q_LORA)), _full(wq.shape),
                  _full((1, KV_LORA)), _full(wk.shape), _full(wv.shape), tile(HEAD_PAD), tile(HEAD_PAD)],
        out_specs=[head_out(HEADS)] * 3,
        out_shape=[jax.ShapeDtypeStruct((b, HEADS, s, HEAD_PAD), BF16)] * 3,
        compiler_params=_params("arbitrary", "arbitrary"),
        name="mla_proj",
    )(x, g.reshape(1, d), win, qn.reshape(1, -1), wq, kvn.reshape(1, -1), wk, wv, cosq, sinq)


def _attn_kernel(q_ref, k_ref, v_ref, o_ref, sa_ref, sb_ref, m_ref, acc_ref, *, tq, nq):
    row = lax.broadcasted_iota(jnp.int32, (tq, tq), 0)
    col = lax.broadcasted_iota(jnp.int32, (tq, tq), 1)
    causal = row >= col
    lane = lax.broadcasted_iota(jnp.int32, (tq, LANES), 1)
    reps = tq // LANES

    def produce(dst_ref, qi, kb):
        q0 = pl.multiple_of(qi * tq, tq)
        k0 = pl.multiple_of(kb * tq, tq)
        for hh in range(ATT_HEADS):
            dst_ref[hh] = _dot_nt(q_ref[0, hh, pl.ds(q0, tq), :], k_ref[0, hh, pl.ds(k0, tq), :])

    def consume(src_ref, kb, masked):
        k0 = pl.multiple_of(kb * tq, tq)
        for hh in range(ATT_HEADS):
            sc = src_ref[hh]
            if masked:
                sc = jnp.where(causal, sc, NEG_BIG)
            m = m_ref[hh]
            m_new = jnp.maximum(m, jnp.max(sc, axis=-1, keepdims=True))
            p = jnp.exp2(sc - jnp.tile(m_new, (1, reps))).astype(BF16)
            acc_ref[hh] = jnp.exp2(m - m_new) * acc_ref[hh] + _dot(p, v_ref[0, hh, pl.ds(k0, tq), :])
            m_ref[hh] = m_new

    def reset():
        m_ref[...] = jnp.full(m_ref.shape, NEG_BIG, F32)
        acc_ref[...] = jnp.zeros(acc_ref.shape, F32)

    def finalize(qi):
        q0 = pl.multiple_of(qi * tq, tq)
        for pair in range(ATT_HEADS // 2):
            acc0 = acc_ref[2 * pair]
            acc1 = acc_ref[2 * pair + 1]
            o0 = acc0 / acc0[:, V_HEAD:V_HEAD + 1]
            o1 = acc1 / acc1[:, V_HEAD:V_HEAD + 1]
            o_ref[0, pl.ds(q0, tq), pair * LANES:(pair + 1) * LANES] = jnp.where(
                lane < V_HEAD, o0, pltpu.roll(o1, V_HEAD, 1)).astype(BF16)

    n_steps = nq * (nq + 1) // 2
    bufs = (sa_ref, sb_ref)
    reset()
    produce(sa_ref, 0, 0)

    def body(t, carry):
        qi, kb = carry
        last = kb == qi
        nqi = jnp.where(last, qi + 1, qi)
        nkb = jnp.where(last, 0, kb + 1)
        for par in range(2):
            src, dst = bufs[par], bufs[1 - par]

            @pl.when((t % 2 == par) & jnp.logical_not(last))
            def _():
                produce(dst, nqi, nkb)
                consume(src, kb, False)

            @pl.when((t % 2 == par) & last)
            def _():
                produce(dst, nqi, nkb)
                consume(src, kb, True)
                finalize(qi)
                reset()

        return nqi, nkb

    lax.fori_loop(0, n_steps - 1, body, (jnp.int32(0), jnp.int32(0)))
    consume(bufs[(n_steps - 1) % 2], nq - 1, True)
    finalize(nq - 1)


def _attention(q, k, v):
    b, _, s, _ = q.shape
    tq = min(ATT_TILE, s)
    spec = pl.BlockSpec((1, ATT_HEADS, s, HEAD_PAD), lambda i, hg: (i, hg, 0, 0))
    return pl.pallas_call(
        functools.partial(_attn_kernel, tq=tq, nq=s // tq),
        grid=(b, HEADS // ATT_HEADS),
        in_specs=[spec, spec, spec],
        out_specs=pl.BlockSpec((1, s, ATT_HEADS * V_HEAD), lambda i, hg: (i, 0, hg)),
        out_shape=jax.ShapeDtypeStruct((b, s, MIX), BF16),
        scratch_shapes=[pltpu.VMEM((ATT_HEADS, tq, tq), F32), pltpu.VMEM((ATT_HEADS, tq, tq), F32),
                        pltpu.VMEM((ATT_HEADS, tq, LANES), F32), pltpu.VMEM((ATT_HEADS, tq, HEAD_PAD), F32)],
        compiler_params=_params("arbitrary", "arbitrary"),
        name="mla_attention",
    )(q, k, v)


def _pool_kernel(x_ref, g_ref, win_ref, wp_ref, ps_ref, o_ref, ext_ref, *, ts):
    si = pl.program_id(1)

    @pl.when(si == 0)
    def _():
        ext_ref[0:POOL_HALO, :] = jnp.zeros((POOL_HALO, MIX), F32)

    h = _rmsnorm(x_ref[0], g_ref[...]).astype(BF16)
    u = _dot(h, win_ref[...])
    ext_ref[POOL_HALO:POOL_HALO + ts, :] = u
    t = si * ts + lax.broadcasted_iota(jnp.int32, (ts, 1), 0)
    for g, w in enumerate(POOL_WINDOWS):
        lo = g * POOL_GROUP
        u_g = u[:, lo:lo + POOL_GROUP]
        win = u_g
        for kk in range(1, w):
            win = win + ext_ref[POOL_HALO - kk:POOL_HALO - kk + ts, lo:lo + POOL_GROUP]
        count = jnp.minimum(t + 1, w).astype(F32)
        d = win / count - u_g
        y = _dot(d.astype(BF16), wp_ref[g])
        o_ref[0, :, lo:lo + POOL_GROUP] = (y * ps_ref[:, lo:lo + POOL_GROUP]).astype(BF16)
    ext_ref[0:POOL_HALO, :] = ext_ref[ts:ts + POOL_HALO, :]


def _pool(x, g, win, wp, ps):
    b, s, d = x.shape
    ts = min(SEQ_TILE, s)
    return pl.pallas_call(
        functools.partial(_pool_kernel, ts=ts),
        grid=(b, s // ts),
        in_specs=[pl.BlockSpec((1, ts, d), lambda i, j: (i, j, 0)), _full((1, d)), _full(win.shape),
                  _full(wp.shape), _full((1, MIX))],
        out_specs=pl.BlockSpec((1, ts, MIX), lambda i, j: (i, j, 0)),
        out_shape=jax.ShapeDtypeStruct((b, s, MIX), BF16),
        scratch_shapes=[pltpu.VMEM((POOL_HALO + ts, MIX), F32)],
        compiler_params=_params("arbitrary", "arbitrary"),
        name="pool_mixer",
    )(x, g.reshape(1, d), win, wp, ps.reshape(1, MIX))


def _causal_conv(ext_ref, u, cw_ref, cb_ref, ts, first):
    width = ext_ref.shape[1]

    @pl.when(first)
    def _():
        ext_ref[0:CONV_HALO, :] = jnp.zeros((CONV_HALO, width), F32)

    ext_ref[CONV_HALO:CONV_HALO + ts, :] = u
    acc = u * cw_ref[CONV_WIDTH - 1:CONV_WIDTH, :] + cb_ref[...]
    for j in range(CONV_WIDTH - 1):
        back = CONV_WIDTH - 1 - j
        acc = acc + ext_ref[CONV_HALO - back:CONV_HALO - back + ts, :] * cw_ref[j:j + 1, :]
    ext_ref[0:CONV_HALO, :] = ext_ref[ts:ts + CONV_HALO, :]
    return acc


SSD_IN_COLS = MIX + SSD_XBC + LANES
SSD_GROUP_COLS = MIX // SSD_GROUPS
SSD_GROUP_HEADS = SSD_HEADS // SSD_GROUPS


def _ssd_kernel(x_ref, g_ref, win_ref, cw_ref, cb_ref, dtb_ref, alog_ref, dskip_ref, ng_ref,
                tri_ref, expand_ref, bd_ref, o_ref, ext_ref, xbc_ref, state_ref, *, ts):
    si = pl.program_id(1)

    @pl.when(si == 0)
    def _():
        state_ref[...] = jnp.zeros(state_ref.shape, F32)

    h = _rmsnorm(x_ref[0], g_ref[...]).astype(BF16)
    u = _dot(h, win_ref[...])
    xbc_ref[...] = _silu(_causal_conv(ext_ref, u[:, MIX:MIX + SSD_XBC], cw_ref, cb_ref, ts, si == 0))
    z = u[:, :MIX]
    dt_all = _softplus(u[:, MIX + SSD_XBC:] + dtb_ref[...])
    a_all = dt_all * (-jnp.exp(alog_ref[...]))
    lc = SSD_CHUNK
    row = lax.broadcasted_iota(jnp.int32, (lc, lc), 0)
    col = lax.broadcasted_iota(jnp.int32, (lc, lc), 1)
    tril = row >= col
    lane = lax.broadcasted_iota(jnp.int32, (1, LANES), 1)
    ys = []
    for c in range(ts // lc):
        r0 = c * lc
        xs = xbc_ref[r0:r0 + lc, 0:MIX]
        bm = xbc_ref[r0:r0 + lc, MIX:MIX + LANES]
        cm = xbc_ref[r0:r0 + lc, MIX + LANES:MIX + 2 * LANES]
        dt = dt_all[r0:r0 + lc]
        cs = _sel_dot_left(tri_ref[...], a_all[r0:r0 + lc])
        cs_t = cs.T
        cs_end = cs[lc - 1:lc, :]
        per_head = jnp.concatenate([dt, jnp.exp(cs), dt * jnp.exp(cs_end - cs)], axis=0)
        wide = _sel_dot(per_head, expand_ref[...])
        dt_x, ecs_x, dd_x = wide[0:lc], wide[lc:2 * lc], wide[2 * lc:3 * lc]
        xdt = (xs * dt_x).astype(BF16)
        xdd = (xs * dd_x).astype(BF16)
        y_parts = []
        for grp in range(SSD_GROUPS):
            in_grp = (lane >= grp * SSD_STATE) & (lane < (grp + 1) * SSD_STATE)
            cm_g = jnp.where(in_grp, cm, 0.0).astype(BF16)
            bm_g = jnp.where(in_grp, bm, 0.0).astype(BF16)
            cb = _dot_nt(cm_g, bm_g)
            gc = grp * SSD_GROUP_COLS
            masks = []
            for hl in range(SSD_GROUP_HEADS):
                hd = grp * SSD_GROUP_HEADS + hl
                seg = jnp.where(tril, cs[:, hd:hd + 1] - cs_t[hd:hd + 1, :], NEG_BIG)
                masks.append((cb * jnp.exp(seg)).astype(BF16))
            m_cat = jnp.concatenate(masks, axis=1)
            x_bd = jnp.tile(xdt[:, gc:gc + SSD_GROUP_COLS], (SSD_GROUP_HEADS, 1)) * bd_ref[...]
            y_diag = _dot(m_cat, x_bd)
            st = state_ref[grp]
            y_off = _dot(cm_g, st.astype(BF16)) * ecs_x[:, gc:gc + SSD_GROUP_COLS]
            state_ref[grp] = (st * ecs_x[lc - 1:lc, gc:gc + SSD_GROUP_COLS]
                              + _dot_tn(bm_g, xdd[:, gc:gc + SSD_GROUP_COLS]))
            y_parts.append(y_diag + y_off)
        ys.append(jnp.concatenate(y_parts, axis=1) + xs * dskip_ref[...])
    y = jnp.concatenate(ys, axis=0) * _silu(z)
    o_ref[0] = _rmsnorm(y, ng_ref[...]).astype(BF16)


def _sel_dot_left(sel, a):
    hi, mid, lo = _split3(a)
    return _dot(sel, hi) + _dot(sel, mid) + _dot(sel, lo)


def _ssd_constants():
    lc = SSD_CHUNK
    tri = (jnp.arange(lc)[:, None] >= jnp.arange(lc)[None, :]).astype(BF16)
    expand = (jnp.arange(LANES)[:, None] == (jnp.arange(MIX)[None, :] // SSD_HEADDIM)).astype(BF16)
    rows = jnp.arange(SSD_GROUP_HEADS * lc)[:, None] // lc
    cols = jnp.arange(SSD_GROUP_COLS)[None, :] // SSD_HEADDIM
    bd = (rows == cols).astype(BF16)
    return tri, expand, bd


def _ssd(x, g, win, cw, cb, dtb, alog, dskip, ng):
    b, s, d = x.shape
    ts = min(SEQ_TILE, s)
    tri, expand, bd = _ssd_constants()
    pad = lambda v: jnp.pad(v, (0, LANES - SSD_HEADS)).reshape(1, LANES)
    return pl.pallas_call(
        functools.partial(_ssd_kernel, ts=ts),
        grid=(b, s // ts),
        in_specs=[pl.BlockSpec((1, ts, d), lambda i, j: (i, j, 0)), _full((1, d)), _full(win.shape),
                  _full((CONV_WIDTH, SSD_XBC)), _full((1, SSD_XBC)), _full((1, LANES)), _full((1, LANES)),
                  _full((1, MIX)), _full((1, MIX)), _full(tri.shape), _full(expand.shape), _full(bd.shape)],
        out_specs=pl.BlockSpec((1, ts, MIX), lambda i, j: (i, j, 0)),
        out_shape=jax.ShapeDtypeStruct((b, s, MIX), BF16),
        scratch_shapes=[pltpu.VMEM((CONV_HALO + ts, SSD_XBC), F32),
                        pltpu.VMEM((ts, SSD_XBC), F32),
                        pltpu.VMEM((SSD_GROUPS, LANES, SSD_GROUP_COLS), F32)],
        compiler_params=_params("arbitrary", "arbitrary"),
        name="ssd_mixer",
    )(x, g.reshape(1, d), win, cw, cb.reshape(1, -1), pad(dtb), pad(alog),
      jnp.repeat(dskip, SSD_HEADDIM).reshape(1, MIX), ng.reshape(1, MIX), tri, expand, bd)


def _lru_kernel(x_ref, g_ref, win_ref, cw_ref, cb_ref, wai_ref, bai_ref, lam_ref, o_ref,
                ext_ref, a_ref, h_ref, carry_ref, *, ts):
    si = pl.program_id(1)

    @pl.when(si == 0)
    def _():
        carry_ref[...] = jnp.zeros(carry_ref.shape, F32)

    hn = _rmsnorm(x_ref[0], g_ref[...]).astype(BF16)
    u = _dot(hn, win_ref[...])
    xc = _causal_conv(ext_ref, u[:, MIX:], cw_ref, cb_ref, ts, si == 0)
    ri = _dot(xc.astype(BF16), wai_ref[...]) + bai_ref[...]
    r_t = _sigmoid(ri[:, :MIX])
    i_t = _sigmoid(ri[:, MIX:])
    log_a = (-LRU_C) * r_t * _softplus(-lam_ref[...])
    a_t = jnp.exp(log_a)
    mult = jnp.sqrt(jnp.tanh(-log_a) * (a_t * a_t + 1.0))
    a_ref[...] = a_t
    h_ref[...] = xc * i_t * mult
    sub = lax.broadcasted_iota(jnp.int32, (SUBLANES, MIX), 0)

    def block(j, carry):
        r0 = pl.multiple_of(j * SUBLANES, SUBLANES)
        a8 = a_ref[pl.ds(r0, SUBLANES), :]
        u8 = h_ref[pl.ds(r0, SUBLANES), :]
        for kk in (1, 2, 4):
            keep = sub >= kk
            u8 = jnp.where(keep, a8 * pltpu.roll(u8, kk, 0) + u8, u8)
            a8 = jnp.where(keep, a8 * pltpu.roll(a8, kk, 0), a8)
        h8 = u8 + a8 * carry
        h_ref[pl.ds(r0, SUBLANES), :] = h8
        return h8[SUBLANES - 1:SUBLANES, :]

    carry_ref[...] = lax.fori_loop(0, ts // SUBLANES, block, carry_ref[...])
    o_ref[0] = (h_ref[...] * _gelu_tanh(u[:, :MIX])).astype(BF16)


def _block_diag(w):
    nb, n, _ = w.shape
    eye = jnp.eye(nb, dtype=w.dtype)
    return (eye[:, None, :, None] * w[:, :, None, :]).reshape(nb * n, nb * n)


def _lru(x, g, win, cw, cb, wai, bai, lam):
    b, s, d = x.shape
    ts = min(SEQ_TILE, s)
    return pl.pallas_call(
        functools.partial(_lru_kernel, ts=ts),
        grid=(b, s // ts),
        in_specs=[pl.BlockSpec((1, ts, d), lambda i, j: (i, j, 0)), _full((1, d)), _full(win.shape),
                  _full((CONV_WIDTH, MIX)), _full((1, MIX)), _full(wai.shape), _full((1, 2 * MIX)), _full((1, MIX))],
        out_specs=pl.BlockSpec((1, ts, MIX), lambda i, j: (i, j, 0)),
        out_shape=jax.ShapeDtypeStruct((b, s, MIX), BF16),
        scratch_shapes=[pltpu.VMEM((CONV_HALO + ts, MIX), F32), pltpu.VMEM((ts, MIX), F32),
                        pltpu.VMEM((ts, MIX), F32), pltpu.VMEM((1, MIX), F32)],
        compiler_params=_params("arbitrary", "arbitrary"),
        name="rglru_mixer",
    )(x, g.reshape(1, d), win, cw, cb.reshape(1, MIX), wai, bai.reshape(1, 2 * MIX), lam.reshape(1, MIX))


def _merge_kernel(x_ref, g_ref, wg_ref, ya_ref, yb_ref, yc_ref, yd_ref, wb_ref, wo_ref, o_ref):
    x = x_ref[...]
    h = _rmsnorm(x, g_ref[...]).astype(BF16)
    merged = None
    for n, y_ref in enumerate((ya_ref, yb_ref, yc_ref, yd_ref)):
        gate = _sigmoid(_dot(h, wg_ref[:, n * D_MODEL:(n + 1) * D_MODEL]))
        term = gate * _dot(y_ref[...], wb_ref[n])
        merged = term if merged is None else merged + term
    o_ref[...] = x + _dot(merged.astype(BF16), wo_ref[...])


def _merge(x2, g, wg, ys, wb, wo):
    t, d = x2.shape
    tm = min(TOK_TILE, t)
    row = lambda w: pl.BlockSpec((tm, w), lambda i: (i, 0))
    return pl.pallas_call(
        _merge_kernel,
        grid=(t // tm,),
        in_specs=[row(d), _full((1, d)), _full(wg.shape), row(MIX), row(MIX), row(MIX), row(MIX),
                  _full(wb.shape), _full(wo.shape)],
        out_specs=row(d),
        out_shape=jax.ShapeDtypeStruct((t, d), F32),
        compiler_params=_params("arbitrary"),
        name="gated_merge",
    )(x2, g.reshape(1, d), wg, *ys, wb, wo)


FF_CHUNK = 1024


def _mlp_ple_kernel(x_ref, g_ref, w1_ref, w2_ref, gp_ref, wpg_ref, p_ref, wple_ref, gf_ref, o_ref, *, final_norm):
    x = x_ref[...]
    h = _rmsnorm(x, g_ref[...]).astype(BF16)
    acc = x
    for c in range(D_FF // FF_CHUNK):
        hid = jnp.maximum(_dot(h, w1_ref[:, c * FF_CHUNK:(c + 1) * FF_CHUNK]), 0.0)
        acc = acc + _dot((hid * hid).astype(BF16), w2_ref[c * FF_CHUNK:(c + 1) * FF_CHUNK, :])
    gate = _sigmoid(_dot(_rmsnorm(acc, gp_ref[...]).astype(BF16), wpg_ref[...]))
    y = acc + _dot(p_ref[...].astype(BF16), wple_ref[...]) * gate
    if final_norm:
        y = _rmsnorm(y, gf_ref[...])
    o_ref[...] = y


def _mlp_ple(x2, g, w1, w2, gp, wpg, p2, wple, g_final, final_norm):
    t, d = x2.shape
    tm = min(TOK_TILE, t)
    row = lambda w: pl.BlockSpec((tm, w), lambda i: (i, 0))
    return pl.pallas_call(
        functools.partial(_mlp_ple_kernel, final_norm=final_norm),
        grid=(t // tm,),
        in_specs=[row(d), _full((1, d)), _resident(w1.shape), _resident(w2.shape), _full((1, d)),
                  _resident(wpg.shape), row(PLE_DIM), _resident(wple.shape), _full((1, d))],
        out_specs=row(d),
        out_shape=jax.ShapeDtypeStruct((t, d), F32),
        compiler_params=_params("arbitrary"),
        name="mlp_ple",
    )(x2, g.reshape(1, d), w1, w2, gp.reshape(1, d), wpg, p2, wple, g_final.reshape(1, d))


_SPLIT = (Q_LORA, KV_LORA, QK_ROPE, MIX, MIX, SSD_XBC, SSD_HEADS, MIX, MIX, N_BRANCH * D_MODEL)


def _col_offsets():
    offs, acc = [], 0
    for sz in _SPLIT:
        offs.append(acc)
        acc += sz
    return offs


def kernel(x, p, positions, g_mix, w_in, q_norm, w_uq, kv_norm, w_ukv, w_pool, pool_scale,
           ssd_conv_w, ssd_conv_b, ssd_dt_bias, ssd_a_log, ssd_d, ssd_norm,
           lru_conv_w, lru_conv_b, lru_w_a, lru_b_a, lru_w_i, lru_b_i, lru_lambda,
           w_branch, w_out, g_mlp, w_ff1, w_ff2, g_ple, w_ple_gate, w_ple, g_final):
    b, s, d = x.shape
    depth = w_in.shape[0]
    t = b * s
    offs = _col_offsets()
    o_pool, o_z, o_xbc, o_dt, o_lg, o_lx, o_gate = offs[3], offs[4], offs[5], offs[6], offs[7], offs[8], offs[9]
    cosq, sinq = _rope_tables(positions)
    for l in range(depth):
        wl = w_in[l]
        g = g_mix[l]
        win_a, wq, wk, wv = _mla_weights(wl, w_uq[l], w_ukv[l])
        q, k, v = _mla_proj(x, g, win_a, q_norm[l], wq, kv_norm[l], wk, wv, cosq, sinq)
        y_a = _attention(q, k, v)
        y_b = _pool(x, g, wl[:, o_pool:o_pool + MIX].astype(BF16), w_pool[l].astype(BF16), pool_scale[l])
        w_dt = jnp.pad(wl[:, o_dt:o_dt + SSD_HEADS], ((0, 0), (0, LANES - SSD_HEADS)))
        win_c = jnp.concatenate([wl[:, o_z:o_z + MIX], wl[:, o_xbc:o_xbc + SSD_XBC], w_dt], axis=1).astype(BF16)
        y_c = _ssd(x, g, win_c, ssd_conv_w[l], ssd_conv_b[l], ssd_dt_bias[l], ssd_a_log[l], ssd_d[l], ssd_norm[l])
        win_d = wl[:, o_lg:o_lg + 2 * MIX].astype(BF16)
        wai = jnp.concatenate([_block_diag(lru_w_a[l]), _block_diag(lru_w_i[l])], axis=1).astype(BF16)
        bai = jnp.concatenate([lru_b_a[l], lru_b_i[l]])
        y_d = _lru(x, g, win_d, lru_conv_w[l], lru_conv_b[l], wai, bai, lru_lambda[l])
        ys = [y.reshape(t, MIX) for y in (y_a, y_b, y_c, y_d)]
        x2 = _merge(x.reshape(t, d), g, wl[:, o_gate:].astype(BF16), ys, w_branch[l].astype(BF16),
                    w_out[l].astype(BF16))
        x2 = _mlp_ple(x2, g_mlp[l], w_ff1[l].astype(BF16), w_ff2[l].astype(BF16), g_ple[l],
                      w_ple_gate[l].astype(BF16), p[l].reshape(t, PLE_DIM), w_ple[l].astype(BF16), g_final,
                      final_norm=(l == depth - 1))
        x = x2.reshape(b, s, d)
    return x
```

```python
import functools

import jax
import jax.numpy as jnp
import numpy as np
from jax import lax
from jax.experimental import pallas as pl
from jax.experimental.pallas import tpu as pltpu

F32 = jnp.float32
BF16 = jnp.bfloat16

D_MODEL = 1024
MIX = 512
N_BRANCH = 4
HEADS = 8
QK_NOPE = 64
QK_ROPE = 32
V_HEAD = 64
Q_LORA = 384
KV_LORA = 256
ROPE_THETA = 10000.0
POOL_WINDOWS = (2, 4, 8, 16)
POOL_GROUP = 128
POOL_HALO = 16
SSD_HEADS = 8
SSD_HEADDIM = 64
SSD_GROUPS = 2
SSD_STATE = 64
SSD_CHUNK = 128
CONV_WIDTH = 4
CONV_HALO = 8
SSD_XBC = 768
LRU_BLOCKS = 8
LRU_BLOCK = 64
LRU_C = 8.0
D_FF = 4096
PLE_DIM = 256
EPS = 1e-6
LANES = 128
SUBLANES = 8
HEAD_PAD = 128
NEG_BIG = -1e30
LOG2_E = 1.4426950408889634
VMEM_LIMIT = 56 * 1024 * 1024

SEQ_TILE = 512
TOK_TILE = 512
ATT_TILE = 512
ATT_HEADS = 4
ATT_SKEW = 1


def _dot(a, b):
    return jnp.dot(a, b, preferred_element_type=F32)


def _dot_nt(a, b):
    return lax.dot_general(a, b, (((1,), (1,)), ((), ())), preferred_element_type=F32)


def _dot_tn(a, b):
    return lax.dot_general(a, b, (((0,), (0,)), ((), ())), preferred_element_type=F32)


def _split3(a):
    hi = a.astype(BF16)
    r1 = a - hi.astype(F32)
    mid = r1.astype(BF16)
    lo = (r1 - mid.astype(F32)).astype(BF16)
    return hi, mid, lo


def _sel_dot2(a, sel):
    hi = a.astype(BF16)
    mid = (a - hi.astype(F32)).astype(BF16)
    return _dot(hi, sel) + _dot(mid, sel)


def _rmsnorm(x, g):
    return x * lax.rsqrt(jnp.mean(x * x, axis=-1, keepdims=True) + EPS) * g


def _sigmoid(x):
    return jax.nn.sigmoid(x)


def _silu(x):
    return x * jax.nn.sigmoid(x)


def _softplus(x):
    return jnp.maximum(x, 0.0) + jnp.log1p(jnp.exp(-jnp.abs(x)))


def _gelu_tanh(x):
    return 0.5 * x * (1.0 + jnp.tanh(0.7978845608028654 * (x + 0.044715 * (x * x * x))))


def _params(*sem):
    return pltpu.CompilerParams(dimension_semantics=sem, vmem_limit_bytes=VMEM_LIMIT)


def _full(shape):
    n = len(shape)
    return pl.BlockSpec(shape, lambda *_: (0,) * n)


def _resident(shape):
    n = len(shape)
    return pl.BlockSpec(shape, lambda *_: (0,) * n, pipeline_mode=pl.Buffered(1))


def _prenorm_kernel(x_ref, g_ref, h_ref):
    h_ref[...] = _rmsnorm(x_ref[...], g_ref[...]).astype(BF16)


def _prenorm(x2, g):
    t, d = x2.shape
    tm = min(TOK_TILE, t)
    row = pl.BlockSpec((tm, d), lambda i: (i, 0))
    return pl.pallas_call(
        _prenorm_kernel,
        grid=(t // tm,),
        in_specs=[row, _full((1, d))],
        out_specs=row,
        out_shape=jax.ShapeDtypeStruct((t, d), BF16),
        compiler_params=_params("arbitrary"),
        name="prenorm",
    )(x2, g.reshape(1, d))


ROPE_HALF = QK_ROPE // 2
ROPE_PACK = LANES // ROPE_HALF


def _rope_kernel(ang_ref, place_ref, cos_ref, sin_ref, *, blk):
    a = ang_ref[...]
    cos_parts = _split3(jnp.cos(a))
    sin_parts = _split3(jnp.sin(a))
    lane = lax.broadcasted_iota(jnp.int32, (1, LANES), 1)
    ones = (lane < QK_NOPE).astype(F32)
    for i in range(ROPE_PACK):
        ci = sum(_dot(part, place_ref[0, i]) for part in cos_parts) + ones
        si = sum(_dot(part, place_ref[1, i]) for part in sin_parts)
        cos_ref[pl.ds(i, blk, stride=ROPE_PACK), :] = ci
        sin_ref[pl.ds(i, blk, stride=ROPE_PACK), :] = si


def _rope_placement():
    place = np.zeros((2, ROPE_PACK, LANES, LANES), np.float32)
    for i in range(ROPE_PACK):
        for j in range(ROPE_HALF):
            src = i * ROPE_HALF + j
            place[0, i, src, QK_NOPE + j] = 1.0
            place[0, i, src, QK_NOPE + ROPE_HALF + j] = 1.0
            place[1, i, src, QK_NOPE + j] = -1.0
            place[1, i, src, QK_NOPE + ROPE_HALF + j] = 1.0
    return jnp.asarray(place, BF16)


def _rope_tables(positions):
    b, s = positions.shape
    t = b * s
    inv = 1.0 / (ROPE_THETA ** (jnp.arange(0, QK_ROPE, 2, dtype=F32) / QK_ROPE))
    rows = t // ROPE_PACK
    pos = positions.astype(F32).reshape(rows, ROPE_PACK)
    ang = jnp.repeat(pos, ROPE_HALF, axis=1) * jnp.tile(inv, ROPE_PACK)[None, :]
    blk = min(rows, 512)
    place = _rope_placement()
    out_spec = pl.BlockSpec((blk * ROPE_PACK, LANES), lambda i: (i, 0))
    cosq, sinq = pl.pallas_call(
        functools.partial(_rope_kernel, blk=blk),
        grid=(rows // blk,),
        in_specs=[pl.BlockSpec((blk, LANES), lambda i: (i, 0)), _full(place.shape)],
        out_specs=[out_spec, out_spec],
        out_shape=[jax.ShapeDtypeStruct((t, LANES), F32)] * 2,
        compiler_params=_params("arbitrary"),
        name="rope_tables",
    )(ang, place)
    return cosq.reshape(b, s, LANES), sinq.reshape(b, s, LANES)


MLA_IN_COLS = Q_LORA + KV_LORA + 2 * HEAD_PAD


def _mla_proj_kernel(h_ref, win_ref, qn_ref, wq_ref, kvn_ref, wk_ref, wv_ref, cos_ref, sin_ref,
                     q_out, k_out, v_out):
    scale = (QK_NOPE + QK_ROPE) ** -0.5 * LOG2_E
    u = _dot(h_ref[0], win_ref[...])
    c_q = u[:, :Q_LORA]
    c_kv = u[:, Q_LORA:Q_LORA + KV_LORA]
    kr = u[:, Q_LORA + KV_LORA:Q_LORA + KV_LORA + HEAD_PAD]
    kr_sw = u[:, Q_LORA + KV_LORA + HEAD_PAD:]
    cosq = cos_ref[0]
    sinq = sin_ref[0]
    cqn = _rmsnorm(c_q, qn_ref[...]).astype(BF16)
    qq = _dot(cqn, wq_ref[...])
    ckvn = _rmsnorm(c_kv, kvn_ref[...]).astype(BF16)
    kn = _dot(ckvn, wk_ref[...])
    v = _dot(ckvn, wv_ref[...])
    lane = lax.broadcasted_iota(jnp.int32, (1, HEAD_PAD), 1)
    ones_lane = (lane == V_HEAD).astype(F32)
    k_rope = kr * cosq + kr_sw * sinq
    wide = HEADS * HEAD_PAD
    for hd in range(HEADS):
        lo = hd * HEAD_PAD
        q_h = (qq[:, lo:lo + HEAD_PAD] * cosq + qq[:, wide + lo:wide + lo + HEAD_PAD] * sinq) * scale
        q_out[0, hd] = q_h.astype(BF16)
        k_out[0, hd] = (kn[:, lo:lo + HEAD_PAD] + k_rope).astype(BF16)
        v_out[0, hd] = (v[:, lo:lo + HEAD_PAD] + ones_lane).astype(BF16)


def _mla_weights(w_in_l, w_uq_l, w_ukv_l):
    half = QK_ROPE // 2
    pad_l, pad_r = QK_NOPE, HEAD_PAD - QK_NOPE - QK_ROPE
    o = Q_LORA + KV_LORA
    w_kr = w_in_l[:, o:o + QK_ROPE]
    w_kr_sw = jnp.concatenate([w_kr[:, half:], w_kr[:, :half]], axis=1)
    place = lambda w: jnp.pad(w, ((0, 0), (pad_l, pad_r)))
    win = jnp.concatenate([w_in_l[:, :o], place(w_kr), place(w_kr_sw)], axis=1).astype(BF16)
    wq3 = w_uq_l.reshape(Q_LORA, HEADS, QK_NOPE + QK_ROPE)
    wq_a = jnp.pad(wq3, ((0, 0), (0, 0), (0, pad_r))).reshape(Q_LORA, HEADS * HEAD_PAD)
    rope = wq3[..., QK_NOPE:]
    rope_sw = jnp.concatenate([rope[..., half:], rope[..., :half]], axis=-1)
    wq_b = jnp.pad(rope_sw, ((0, 0), (0, 0), (pad_l, pad_r))).reshape(Q_LORA, HEADS * HEAD_PAD)
    wq = jnp.concatenate([wq_a, wq_b], axis=1).astype(BF16)
    wkv3 = w_ukv_l.reshape(KV_LORA, HEADS, QK_NOPE + V_HEAD)
    wk = jnp.pad(wkv3[..., :QK_NOPE], ((0, 0), (0, 0), (0, HEAD_PAD - QK_NOPE))).reshape(KV_LORA, HEADS * HEAD_PAD)
    wv = jnp.pad(wkv3[..., QK_NOPE:], ((0, 0), (0, 0), (0, HEAD_PAD - V_HEAD))).reshape(KV_LORA, HEADS * HEAD_PAD)
    return win, wq, wk.astype(BF16), wv.astype(BF16)


def _mla_proj(h, win, qn, wq, kvn, wk, wv, cosq, sinq):
    b, s, d = h.shape
    ts = min(SEQ_TILE, s)
    tile = lambda w: pl.BlockSpec((1, ts, w), lambda i, j: (i, j, 0))
    head_out = lambda n: pl.BlockSpec((1, n, ts, HEAD_PAD), lambda i, j: (i, 0, j, 0))
    return pl.pallas_call(
        _mla_proj_kernel,
        grid=(b, s // ts),
        in_specs=[tile(d), _full(win.shape), _full((1, Q_LORA)), _full(wq.shape),
                  _full((1, KV_LORA)), _full(wk.shape), _full(wv.shape), tile(HEAD_PAD), tile(HEAD_PAD)],
        out_specs=[head_out(HEADS)] * 3,
        out_shape=[jax.ShapeDtypeStruct((b, HEADS, s, HEAD_PAD), BF16)] * 3,
        compiler_params=_params("arbitrary", "arbitrary"),
        name="mla_proj",
    )(h, win, qn.reshape(1, -1), wq, kvn.reshape(1, -1), wk, wv, cosq, sinq)


def _attn_kernel(q_ref, k_ref, v_ref, o_ref, sa_ref, sb_ref, m_ref, acc_ref, *, tq, nq):
    row = lax.broadcasted_iota(jnp.int32, (tq, tq), 0)
    col = lax.broadcasted_iota(jnp.int32, (tq, tq), 1)
    causal = row >= col
    lane = lax.broadcasted_iota(jnp.int32, (tq, LANES), 1)
    reps = tq // LANES

    def produce(dst_ref, hh, qi, kb):
        q0 = pl.multiple_of(qi * tq, tq)
        k0 = pl.multiple_of(kb * tq, tq)
        dst_ref[hh] = _dot_nt(q_ref[0, hh, pl.ds(q0, tq), :], k_ref[0, hh, pl.ds(k0, tq), :])

    def consume(src_ref, hh, kb, masked):
        k0 = pl.multiple_of(kb * tq, tq)
        sc = src_ref[hh]
        if masked:
            sc = jnp.where(causal, sc, NEG_BIG)
        m = m_ref[hh]
        m_new = jnp.maximum(m, jnp.max(sc, axis=-1, keepdims=True))
        p = jnp.exp2(sc - jnp.tile(m_new, (1, reps))).astype(BF16)
        acc_ref[hh] = jnp.exp2(m - m_new) * acc_ref[hh] + _dot(p, v_ref[0, hh, pl.ds(k0, tq), :])
        m_ref[hh] = m_new

    def step(src_ref, dst_ref, kb, masked, nqi, nkb):
        for hh in range(min(ATT_SKEW, ATT_HEADS)):
            produce(dst_ref, hh, nqi, nkb)
        for hh in range(ATT_HEADS):
            consume(src_ref, hh, kb, masked)
            if hh + ATT_SKEW < ATT_HEADS:
                produce(dst_ref, hh + ATT_SKEW, nqi, nkb)

    def reset():
        m_ref[...] = jnp.full(m_ref.shape, NEG_BIG, F32)
        acc_ref[...] = jnp.zeros(acc_ref.shape, F32)

    def finalize(qi):
        q0 = pl.multiple_of(qi * tq, tq)
        for pair in range(ATT_HEADS // 2):
            acc0 = acc_ref[2 * pair]
            acc1 = acc_ref[2 * pair + 1]
            o0 = acc0 / acc0[:, V_HEAD:V_HEAD + 1]
            o1 = acc1 / acc1[:, V_HEAD:V_HEAD + 1]
            o_ref[0, pl.ds(q0, tq), pair * LANES:(pair + 1) * LANES] = jnp.where(
                lane < V_HEAD, o0, pltpu.roll(o1, V_HEAD, 1)).astype(BF16)

    n_steps = nq * (nq + 1) // 2
    bufs = (sa_ref, sb_ref)
    reset()
    for hh in range(ATT_HEADS):
        produce(sa_ref, hh, 0, 0)

    def body(t, carry):
        qi, kb = carry
        last = kb == qi
        nqi = jnp.where(last, qi + 1, qi)
        nkb = jnp.where(last, 0, kb + 1)
        for par in range(2):
            src, dst = bufs[par], bufs[1 - par]

            @pl.when((t % 2 == par) & jnp.logical_not(last))
            def _():
                step(src, dst, kb, False, nqi, nkb)

            @pl.when((t % 2 == par) & last)
            def _():
                step(src, dst, kb, True, nqi, nkb)
                finalize(qi)
                reset()

        return nqi, nkb

    lax.fori_loop(0, n_steps - 1, body, (jnp.int32(0), jnp.int32(0)))
    for hh in range(ATT_HEADS):
        consume(bufs[(n_steps - 1) % 2], hh, nq - 1, True)
    finalize(nq - 1)


def _attention(q, k, v):
    b, _, s, _ = q.shape
    tq = min(ATT_TILE, s)
    spec = pl.BlockSpec((1, ATT_HEADS, s, HEAD_PAD), lambda i, hg: (i, hg, 0, 0))
    return pl.pallas_call(
        functools.partial(_attn_kernel, tq=tq, nq=s // tq),
        grid=(b, HEADS // ATT_HEADS),
        in_specs=[spec, spec, spec],
        out_specs=pl.BlockSpec((1, s, ATT_HEADS * V_HEAD), lambda i, hg: (i, 0, hg)),
        out_shape=jax.ShapeDtypeStruct((b, s, MIX), BF16),
        scratch_shapes=[pltpu.VMEM((ATT_HEADS, tq, tq), F32), pltpu.VMEM((ATT_HEADS, tq, tq), F32),
                        pltpu.VMEM((ATT_HEADS, tq, LANES), F32), pltpu.VMEM((ATT_HEADS, tq, HEAD_PAD), F32)],
        compiler_params=_params("arbitrary", "arbitrary"),
        name="mla_attention",
    )(q, k, v)


def _pool_kernel(h_ref, win_ref, wp_ref, ps_ref, o_ref, ext_ref, *, ts):
    si = pl.program_id(1)

    @pl.when(si == 0)
    def _():
        ext_ref[0:POOL_HALO, :] = jnp.zeros((POOL_HALO, MIX), F32)

    u = _dot(h_ref[0], win_ref[...])
    ext_ref[POOL_HALO:POOL_HALO + ts, :] = u
    t = si * ts + lax.broadcasted_iota(jnp.int32, (ts, 1), 0)
    for g, w in enumerate(POOL_WINDOWS):
        lo = g * POOL_GROUP
        u_g = u[:, lo:lo + POOL_GROUP]
        win = u_g
        for kk in range(1, w):
            win = win + ext_ref[POOL_HALO - kk:POOL_HALO - kk + ts, lo:lo + POOL_GROUP]
        count = jnp.minimum(t + 1, w).astype(F32)
        d = win / count - u_g
        y = _dot(d.astype(BF16), wp_ref[g])
        o_ref[0, :, lo:lo + POOL_GROUP] = (y * ps_ref[:, lo:lo + POOL_GROUP]).astype(BF16)
    ext_ref[0:POOL_HALO, :] = ext_ref[ts:ts + POOL_HALO, :]


def _pool(h, win, wp, ps):
    b, s, d = h.shape
    ts = min(SEQ_TILE, s)
    return pl.pallas_call(
        functools.partial(_pool_kernel, ts=ts),
        grid=(b, s // ts),
        in_specs=[pl.BlockSpec((1, ts, d), lambda i, j: (i, j, 0)), _full(win.shape),
                  _full(wp.shape), _full((1, MIX))],
        out_specs=pl.BlockSpec((1, ts, MIX), lambda i, j: (i, j, 0)),
        out_shape=jax.ShapeDtypeStruct((b, s, MIX), BF16),
        scratch_shapes=[pltpu.VMEM((POOL_HALO + ts, MIX), F32)],
        compiler_params=_params("arbitrary", "arbitrary"),
        name="pool_mixer",
    )(h, win, wp, ps.reshape(1, MIX))


def _causal_conv(ext_ref, u, cw_ref, cb_ref, ts, first):
    width = ext_ref.shape[1]

    @pl.when(first)
    def _():
        ext_ref[0:CONV_HALO, :] = jnp.zeros((CONV_HALO, width), F32)

    ext_ref[CONV_HALO:CONV_HALO + ts, :] = u
    acc = u * cw_ref[CONV_WIDTH - 1:CONV_WIDTH, :] + cb_ref[...]
    for j in range(CONV_WIDTH - 1):
        back = CONV_WIDTH - 1 - j
        acc = acc + ext_ref[CONV_HALO - back:CONV_HALO - back + ts, :] * cw_ref[j:j + 1, :]
    ext_ref[0:CONV_HALO, :] = ext_ref[ts:ts + CONV_HALO, :]
    return acc


SSD_IN_COLS = MIX + SSD_XBC + LANES
SSD_GROUP_COLS = MIX // SSD_GROUPS
SSD_GROUP_HEADS = SSD_HEADS // SSD_GROUPS


def _ssd_kernel(h_ref, win_ref, cw_ref, cb_ref, dtb_ref, alog_ref, dskip_ref, ng_ref,
                tri_ref, expand_ref, bd_ref, o_ref, ext_ref, xbc_ref, state_ref, *, ts):
    si = pl.program_id(1)

    @pl.when(si == 0)
    def _():
        state_ref[...] = jnp.zeros(state_ref.shape, F32)

    u = _dot(h_ref[0], win_ref[...])
    xbc_ref[...] = _silu(_causal_conv(ext_ref, u[:, MIX:MIX + SSD_XBC], cw_ref, cb_ref, ts, si == 0))
    z = u[:, :MIX]
    dt_all = _softplus(u[:, MIX + SSD_XBC:] + dtb_ref[...])
    a_all = dt_all * (-jnp.exp(alog_ref[...]))
    lc = SSD_CHUNK
    row = lax.broadcasted_iota(jnp.int32, (lc, lc), 0)
    col = lax.broadcasted_iota(jnp.int32, (lc, lc), 1)
    tril = row >= col
    lane = lax.broadcasted_iota(jnp.int32, (1, LANES), 1)
    ys = []
    for c in range(ts // lc):
        r0 = c * lc
        xs = xbc_ref[r0:r0 + lc, 0:MIX]
        bm = xbc_ref[r0:r0 + lc, MIX:MIX + LANES]
        cm = xbc_ref[r0:r0 + lc, MIX + LANES:MIX + 2 * LANES]
        dt = dt_all[r0:r0 + lc]
        cs = _sel_dot_left(tri_ref[...], a_all[r0:r0 + lc])
        cs_t = cs.T
        dt_t = dt.T
        cs_end = cs[lc - 1:lc, :]
        per_head = jnp.concatenate([jnp.exp(cs), dt * jnp.exp(cs_end - cs)], axis=0)
        wide = _sel_dot2(per_head, expand_ref[...])
        ecs_x, dd_x = wide[0:lc], wide[lc:2 * lc]
        xs_b = xs.astype(BF16)
        xdd = (xs * dd_x).astype(BF16)
        y_parts = []
        for grp in range(SSD_GROUPS):
            in_grp = (lane >= grp * SSD_STATE) & (lane < (grp + 1) * SSD_STATE)
            cm_g = jnp.where(in_grp, cm, 0.0).astype(BF16)
            bm_g = jnp.where(in_grp, bm, 0.0).astype(BF16)
            cb = _dot_nt(cm_g, bm_g)
            gc = grp * SSD_GROUP_COLS
            masks = []
            for hl in range(SSD_GROUP_HEADS):
                hd = grp * SSD_GROUP_HEADS + hl
                seg = jnp.where(tril, cs[:, hd:hd + 1] - cs_t[hd:hd + 1, :], NEG_BIG)
                masks.append((cb * jnp.exp(seg) * dt_t[hd:hd + 1, :]).astype(BF16))
            m_cat = jnp.concatenate(masks, axis=1)
            x_bd = jnp.tile(xs_b[:, gc:gc + SSD_GROUP_COLS], (SSD_GROUP_HEADS, 1)) * bd_ref[...]
            y_diag = _dot(m_cat, x_bd)
            st = state_ref[grp]
            y_off = _dot(cm_g, st.astype(BF16)) * ecs_x[:, gc:gc + SSD_GROUP_COLS]
            state_ref[grp] = (st * ecs_x[lc - 1:lc, gc:gc + SSD_GROUP_COLS]
                              + _dot_tn(bm_g, xdd[:, gc:gc + SSD_GROUP_COLS]))
            y_parts.append(y_diag + y_off)
        ys.append(jnp.concatenate(y_parts, axis=1) + xs * dskip_ref[...])
    y = jnp.concatenate(ys, axis=0) * _silu(z)
    o_ref[0] = _rmsnorm(y, ng_ref[...]).astype(BF16)


def _sel_dot_left(sel, a):
    hi, mid, lo = _split3(a)
    return _dot(sel, hi) + _dot(sel, mid) + _dot(sel, lo)


def _ssd_constants():
    lc = SSD_CHUNK
    tri = (jnp.arange(lc)[:, None] >= jnp.arange(lc)[None, :]).astype(BF16)
    expand = (jnp.arange(LANES)[:, None] == (jnp.arange(MIX)[None, :] // SSD_HEADDIM)).astype(BF16)
    rows = jnp.arange(SSD_GROUP_HEADS * lc)[:, None] // lc
    cols = jnp.arange(SSD_GROUP_COLS)[None, :] // SSD_HEADDIM
    bd = (rows == cols).astype(BF16)
    return tri, expand, bd


def _ssd(h, win, cw, cb, dtb, alog, dskip, ng):
    b, s, d = h.shape
    ts = min(SEQ_TILE, s)
    tri, expand, bd = _ssd_constants()
    pad = lambda v: jnp.pad(v, (0, LANES - SSD_HEADS)).reshape(1, LANES)
    return pl.pallas_call(
        functools.partial(_ssd_kernel, ts=ts),
        grid=(b, s // ts),
        in_specs=[pl.BlockSpec((1, ts, d), lambda i, j: (i, j, 0)), _full(win.shape),
                  _full((CONV_WIDTH, SSD_XBC)), _full((1, SSD_XBC)), _full((1, LANES)), _full((1, LANES)),
                  _full((1, MIX)), _full((1, MIX)), _full(tri.shape), _full(expand.shape), _full(bd.shape)],
        out_specs=pl.BlockSpec((1, ts, MIX), lambda i, j: (i, j, 0)),
        out_shape=jax.ShapeDtypeStruct((b, s, MIX), BF16),
        scratch_shapes=[pltpu.VMEM((CONV_HALO + ts, SSD_XBC), F32),
                        pltpu.VMEM((ts, SSD_XBC), F32),
                        pltpu.VMEM((SSD_GROUPS, LANES, SSD_GROUP_COLS), F32)],
        compiler_params=_params("arbitrary", "arbitrary"),
        name="ssd_mixer",
    )(h, win, cw, cb.reshape(1, -1), pad(dtb), pad(alog),
      jnp.repeat(dskip, SSD_HEADDIM).reshape(1, MIX), ng.reshape(1, MIX), tri, expand, bd)


def _lru_kernel(hn_ref, win_ref, cw_ref, cb_ref, wai_ref, bai_ref, lam_ref, o_ref,
                ext_ref, a_ref, h_ref, carry_ref, *, ts):
    si = pl.program_id(1)

    @pl.when(si == 0)
    def _():
        carry_ref[...] = jnp.zeros(carry_ref.shape, F32)

    u = _dot(hn_ref[0], win_ref[...])
    xc = _causal_conv(ext_ref, u[:, MIX:], cw_ref, cb_ref, ts, si == 0)
    ri = _dot(xc.astype(BF16), wai_ref[...]) + bai_ref[...]
    r_t = _sigmoid(ri[:, :MIX])
    i_t = _sigmoid(ri[:, MIX:])
    log_a = (-LRU_C) * r_t * _softplus(-lam_ref[...])
    a_t = jnp.exp(log_a)
    mult = jnp.sqrt(jnp.tanh(-log_a) * (a_t * a_t + 1.0))
    a_ref[...] = a_t
    h_ref[...] = xc * i_t * mult
    sub = lax.broadcasted_iota(jnp.int32, (SUBLANES, MIX), 0)

    def block(j, carry):
        r0 = pl.multiple_of(j * SUBLANES, SUBLANES)
        a8 = a_ref[pl.ds(r0, SUBLANES), :]
        u8 = h_ref[pl.ds(r0, SUBLANES), :]
        for kk in (1, 2, 4):
            keep = sub >= kk
            u8 = jnp.where(keep, a8 * pltpu.roll(u8, kk, 0) + u8, u8)
            a8 = jnp.where(keep, a8 * pltpu.roll(a8, kk, 0), a8)
        h8 = u8 + a8 * carry
        h_ref[pl.ds(r0, SUBLANES), :] = h8
        return h8[SUBLANES - 1:SUBLANES, :]

    carry_ref[...] = lax.fori_loop(0, ts // SUBLANES, block, carry_ref[...])
    o_ref[0] = (h_ref[...] * _gelu_tanh(u[:, :MIX])).astype(BF16)


def _block_diag(w):
    nb, n, _ = w.shape
    eye = jnp.eye(nb, dtype=w.dtype)
    return (eye[:, None, :, None] * w[:, :, None, :]).reshape(nb * n, nb * n)


def _lru(h, win, cw, cb, wai, bai, lam):
    b, s, d = h.shape
    ts = min(SEQ_TILE, s)
    return pl.pallas_call(
        functools.partial(_lru_kernel, ts=ts),
        grid=(b, s // ts),
        in_specs=[pl.BlockSpec((1, ts, d), lambda i, j: (i, j, 0)), _full(win.shape),
                  _full((CONV_WIDTH, MIX)), _full((1, MIX)), _full(wai.shape), _full((1, 2 * MIX)), _full((1, MIX))],
        out_specs=pl.BlockSpec((1, ts, MIX), lambda i, j: (i, j, 0)),
        out_shape=jax.ShapeDtypeStruct((b, s, MIX), BF16),
        scratch_shapes=[pltpu.VMEM((CONV_HALO + ts, MIX), F32), pltpu.VMEM((ts, MIX), F32),
                        pltpu.VMEM((ts, MIX), F32), pltpu.VMEM((1, MIX), F32)],
        compiler_params=_params("arbitrary", "arbitrary"),
        name="rglru_mixer",
    )(h, win, cw, cb.reshape(1, MIX), wai, bai.reshape(1, 2 * MIX), lam.reshape(1, MIX))


def _merge_kernel(x_ref, h_ref, wg_ref, ya_ref, yb_ref, yc_ref, yd_ref, wb_ref, wo_ref, o_ref):
    h = h_ref[...]
    merged = None
    for n, y_ref in enumerate((ya_ref, yb_ref, yc_ref, yd_ref)):
        gate = _sigmoid(_dot(h, wg_ref[:, n * D_MODEL:(n + 1) * D_MODEL]))
        term = gate * _dot(y_ref[...], wb_ref[n])
        merged = term if merged is None else merged + term
    o_ref[...] = x_ref[...] + _dot(merged.astype(BF16), wo_ref[...])


def _merge(x2, h2, wg, ys, wb, wo):
    t, d = x2.shape
    tm = min(TOK_TILE, t)
    row = lambda w: pl.BlockSpec((tm, w), lambda i: (i, 0))
    return pl.pallas_call(
        _merge_kernel,
        grid=(t // tm,),
        in_specs=[row(d), row(d), _full(wg.shape), row(MIX), row(MIX), row(MIX), row(MIX),
                  _full(wb.shape), _full(wo.shape)],
        out_specs=row(d),
        out_shape=jax.ShapeDtypeStruct((t, d), F32),
        compiler_params=_params("arbitrary"),
        name="gated_merge",
    )(x2, h2, wg, *ys, wb, wo)


FF_CHUNK = 1024


def _mlp_ple_kernel(x_ref, g_ref, w1_ref, w2_ref, gp_ref, wpg_ref, p_ref, wple_ref, gn_ref, *out_refs, last):
    x = x_ref[...]
    h = _rmsnorm(x, g_ref[...]).astype(BF16)
    acc = x
    for c in range(D_FF // FF_CHUNK):
        hid = jnp.maximum(_dot(h, w1_ref[:, c * FF_CHUNK:(c + 1) * FF_CHUNK]), 0.0)
        acc = acc + _dot((hid * hid).astype(BF16), w2_ref[c * FF_CHUNK:(c + 1) * FF_CHUNK, :])
    gate = _sigmoid(_dot(_rmsnorm(acc, gp_ref[...]).astype(BF16), wpg_ref[...]))
    y = acc + _dot(p_ref[0].astype(BF16), wple_ref[...]) * gate
    normed = _rmsnorm(y, gn_ref[...])
    if last:
        out_refs[0][...] = normed
    else:
        out_refs[0][...] = y
        out_refs[1][...] = normed.astype(BF16)


def _mlp_ple(x2, g, w1, w2, gp, wpg, p3, layer, wple, g_next, last):
    t, d = x2.shape
    tm = min(TOK_TILE, t)
    row = lambda w: pl.BlockSpec((tm, w), lambda i: (i, 0))
    out_specs = [row(d)] if last else [row(d), row(d)]
    out_shape = [jax.ShapeDtypeStruct((t, d), F32)] + ([] if last else [jax.ShapeDtypeStruct((t, d), BF16)])
    return pl.pallas_call(
        functools.partial(_mlp_ple_kernel, last=last),
        grid=(t // tm,),
        in_specs=[row(d), _full((1, d)), _resident(w1.shape), _resident(w2.shape), _full((1, d)),
                  _resident(wpg.shape), pl.BlockSpec((1, tm, PLE_DIM), lambda i: (layer, i, 0)),
                  _resident(wple.shape), _full((1, d))],
        out_specs=out_specs,
        out_shape=out_shape,
        compiler_params=_params("arbitrary"),
        name="mlp_ple",
    )(x2, g.reshape(1, d), w1, w2, gp.reshape(1, d), wpg, p3, wple, g_next.reshape(1, d))


_SPLIT = (Q_LORA, KV_LORA, QK_ROPE, MIX, MIX, SSD_XBC, SSD_HEADS, MIX, MIX, N_BRANCH * D_MODEL)


def _col_offsets():
    offs, acc = [], 0
    for sz in _SPLIT:
        offs.append(acc)
        acc += sz
    return offs


def kernel(x, p, positions, g_mix, w_in, q_norm, w_uq, kv_norm, w_ukv, w_pool, pool_scale,
           ssd_conv_w, ssd_conv_b, ssd_dt_bias, ssd_a_log, ssd_d, ssd_norm,
           lru_conv_w, lru_conv_b, lru_w_a, lru_b_a, lru_w_i, lru_b_i, lru_lambda,
           w_branch, w_out, g_mlp, w_ff1, w_ff2, g_ple, w_ple_gate, w_ple, g_final):
    b, s, d = x.shape
    depth = w_in.shape[0]
    t = b * s
    offs = _col_offsets()
    o_pool, o_z, o_xbc, o_dt, o_lg, o_lx, o_gate = offs[3], offs[4], offs[5], offs[6], offs[7], offs[8], offs[9]
    cosq, sinq = _rope_tables(positions)
    p3 = p.reshape(depth, t, PLE_DIM)
    x2 = x.reshape(t, d)
    h2 = _prenorm(x2, g_mix[0])
    for l in range(depth):
        wl = w_in[l]
        h = h2.reshape(b, s, d)
        win_a, wq, wk, wv = _mla_weights(wl, w_uq[l], w_ukv[l])
        q, k, v = _mla_proj(h, win_a, q_norm[l], wq, kv_norm[l], wk, wv, cosq, sinq)
        y_a = _attention(q, k, v)
        y_b = _pool(h, wl[:, o_pool:o_pool + MIX].astype(BF16), w_pool[l].astype(BF16), pool_scale[l])
        w_dt = jnp.pad(wl[:, o_dt:o_dt + SSD_HEADS], ((0, 0), (0, LANES - SSD_HEADS)))
        win_c = jnp.concatenate([wl[:, o_z:o_z + MIX], wl[:, o_xbc:o_xbc + SSD_XBC], w_dt], axis=1).astype(BF16)
        y_c = _ssd(h, win_c, ssd_conv_w[l], ssd_conv_b[l], ssd_dt_bias[l], ssd_a_log[l], ssd_d[l], ssd_norm[l])
        win_d = wl[:, o_lg:o_lg + 2 * MIX].astype(BF16)
        wai = jnp.concatenate([_block_diag(lru_w_a[l]), _block_diag(lru_w_i[l])], axis=1).astype(BF16)
        bai = jnp.concatenate([lru_b_a[l], lru_b_i[l]])
        y_d = _lru(h, win_d, lru_conv_w[l], lru_conv_b[l], wai, bai, lru_lambda[l])
        ys = [y.reshape(t, MIX) for y in (y_a, y_b, y_c, y_d)]
        x2 = _merge(x2, h2, wl[:, o_gate:].astype(BF16), ys, w_branch[l].astype(BF16), w_out[l].astype(BF16))
        last = l == depth - 1
        outs = _mlp_ple(x2, g_mlp[l], w_ff1[l].astype(BF16), w_ff2[l].astype(BF16), g_ple[l],
                        w_ple_gate[l].astype(BF16), p3, l, w_ple[l].astype(BF16),
                        g_final if last else g_mix[l + 1], last)
        if last:
            x2 = outs[0]
        else:
            x2, h2 = outs
    return x2.reshape(b, s, d)
```

```python
import functools

import jax
import jax.numpy as jnp
import numpy as np
from jax import lax
from jax.experimental import pallas as pl
from jax.experimental.pallas import tpu as pltpu

F32 = jnp.float32
BF16 = jnp.bfloat16

D_MODEL = 1024
MIX = 512
N_BRANCH = 4
HEADS = 8
QK_NOPE = 64
QK_ROPE = 32
V_HEAD = 64
Q_LORA = 384
KV_LORA = 256
ROPE_THETA = 10000.0
POOL_WINDOWS = (2, 4, 8, 16)
POOL_GROUP = 128
POOL_HALO = 16
SSD_HEADS = 8
SSD_HEADDIM = 64
SSD_GROUPS = 2
SSD_STATE = 64
SSD_CHUNK = 128
CONV_WIDTH = 4
CONV_HALO = 8
SSD_XBC = 768
LRU_BLOCKS = 8
LRU_BLOCK = 64
LRU_C = 8.0
D_FF = 4096
PLE_DIM = 256
EPS = 1e-6
LANES = 128
SUBLANES = 8
HEAD_PAD = 128
NEG_BIG = -1e30
LOG2_E = 1.4426950408889634
VMEM_LIMIT = 56 * 1024 * 1024

SEQ_TILE = 512
POOL_SUB = 4
SSD_SUB = 2
LRU_SUB = 2
NORM_TILE = 2048
TOK_TILE = 512
ATT_TILE = 512
ATT_HEADS = 4
ATT_SKEW = 1


def _dot(a, b):
    return jnp.dot(a, b, preferred_element_type=F32)


def _dot_nt(a, b):
    return lax.dot_general(a, b, (((1,), (1,)), ((), ())), preferred_element_type=F32)


def _dot_tn(a, b):
    return lax.dot_general(a, b, (((0,), (0,)), ((), ())), preferred_element_type=F32)


def _split3(a):
    hi = a.astype(BF16)
    r1 = a - hi.astype(F32)
    mid = r1.astype(BF16)
    lo = (r1 - mid.astype(F32)).astype(BF16)
    return hi, mid, lo


def _sel_dot2(a, sel):
    hi = a.astype(BF16)
    mid = (a - hi.astype(F32)).astype(BF16)
    return _dot(hi, sel) + _dot(mid, sel)


def _rmsnorm(x, g):
    return x * lax.rsqrt(jnp.mean(x * x, axis=-1, keepdims=True) + EPS) * g


def _sigmoid(x):
    return jax.nn.sigmoid(x)


def _silu(x):
    return x * jax.nn.sigmoid(x)


def _softplus(x):
    return jnp.maximum(x, 0.0) + jnp.log1p(jnp.exp(-jnp.abs(x)))


def _gelu_tanh(x):
    return 0.5 * x * (1.0 + jnp.tanh(0.7978845608028654 * (x + 0.044715 * (x * x * x))))


def _params(*sem):
    return pltpu.CompilerParams(dimension_semantics=sem, vmem_limit_bytes=VMEM_LIMIT)


def _full(shape):
    n = len(shape)
    return pl.BlockSpec(shape, lambda *_: (0,) * n)


def _resident(shape):
    n = len(shape)
    return pl.BlockSpec(shape, lambda *_: (0,) * n, pipeline_mode=pl.Buffered(1))


def _prenorm_kernel(x_ref, g_ref, h_ref):
    h_ref[...] = _rmsnorm(x_ref[...], g_ref[...]).astype(BF16)


def _prenorm(x2, g):
    t, d = x2.shape
    tm = min(NORM_TILE, t)
    row = pl.BlockSpec((tm, d), lambda i: (i, 0))
    return pl.pallas_call(
        _prenorm_kernel,
        grid=(t // tm,),
        in_specs=[row, _full((1, d))],
        out_specs=row,
        out_shape=jax.ShapeDtypeStruct((t, d), BF16),
        compiler_params=_params("arbitrary"),
        name="prenorm",
    )(x2, g.reshape(1, d))


ROPE_HALF = QK_ROPE // 2
ROPE_PACK = LANES // ROPE_HALF


def _rope_kernel(ang_ref, place_ref, cos_ref, sin_ref, *, blk):
    a = ang_ref[...]
    cos_parts = _split3(jnp.cos(a))
    sin_parts = _split3(jnp.sin(a))
    lane = lax.broadcasted_iota(jnp.int32, (1, LANES), 1)
    ones = (lane < QK_NOPE).astype(F32)
    for i in range(ROPE_PACK):
        ci = sum(_dot(part, place_ref[0, i]) for part in cos_parts) + ones
        si = sum(_dot(part, place_ref[1, i]) for part in sin_parts)
        cos_ref[pl.ds(i, blk, stride=ROPE_PACK), :] = ci
        sin_ref[pl.ds(i, blk, stride=ROPE_PACK), :] = si


def _rope_placement():
    place = np.zeros((2, ROPE_PACK, LANES, LANES), np.float32)
    for i in range(ROPE_PACK):
        for j in range(ROPE_HALF):
            src = i * ROPE_HALF + j
            place[0, i, src, QK_NOPE + j] = 1.0
            place[0, i, src, QK_NOPE + ROPE_HALF + j] = 1.0
            place[1, i, src, QK_NOPE + j] = -1.0
            place[1, i, src, QK_NOPE + ROPE_HALF + j] = 1.0
    return jnp.asarray(place, BF16)


def _rope_tables(positions):
    b, s = positions.shape
    t = b * s
    inv = 1.0 / (ROPE_THETA ** (jnp.arange(0, QK_ROPE, 2, dtype=F32) / QK_ROPE))
    rows = t // ROPE_PACK
    pos = positions.astype(F32).reshape(rows, ROPE_PACK)
    ang = jnp.repeat(pos, ROPE_HALF, axis=1) * jnp.tile(inv, ROPE_PACK)[None, :]
    blk = min(rows, 512)
    place = _rope_placement()
    out_spec = pl.BlockSpec((blk * ROPE_PACK, LANES), lambda i: (i, 0))
    cosq, sinq = pl.pallas_call(
        functools.partial(_rope_kernel, blk=blk),
        grid=(rows // blk,),
        in_specs=[pl.BlockSpec((blk, LANES), lambda i: (i, 0)), _full(place.shape)],
        out_specs=[out_spec, out_spec],
        out_shape=[jax.ShapeDtypeStruct((t, LANES), F32)] * 2,
        compiler_params=_params("arbitrary"),
        name="rope_tables",
    )(ang, place)
    return cosq.reshape(b, s, LANES), sinq.reshape(b, s, LANES)


MLA_IN_COLS = Q_LORA + KV_LORA + 2 * HEAD_PAD


def _mla_proj_kernel(h_ref, win_ref, qn_ref, wq_ref, kvn_ref, wk_ref, wv_ref, cos_ref, sin_ref,
                     q_out, k_out, v_out):
    scale = (QK_NOPE + QK_ROPE) ** -0.5 * LOG2_E
    u = _dot(h_ref[0], win_ref[...])
    c_q = u[:, :Q_LORA]
    c_kv = u[:, Q_LORA:Q_LORA + KV_LORA]
    kr = u[:, Q_LORA + KV_LORA:Q_LORA + KV_LORA + HEAD_PAD]
    kr_sw = u[:, Q_LORA + KV_LORA + HEAD_PAD:]
    cosq = cos_ref[0]
    sinq = sin_ref[0]
    cqn = _rmsnorm(c_q, qn_ref[...]).astype(BF16)
    qq = _dot(cqn, wq_ref[...])
    ckvn = _rmsnorm(c_kv, kvn_ref[...]).astype(BF16)
    kn = _dot(ckvn, wk_ref[...])
    v = _dot(ckvn, wv_ref[...])
    lane = lax.broadcasted_iota(jnp.int32, (1, HEAD_PAD), 1)
    ones_lane = (lane == V_HEAD).astype(F32)
    k_rope = kr * cosq + kr_sw * sinq
    wide = HEADS * HEAD_PAD
    for hd in range(HEADS):
        lo = hd * HEAD_PAD
        q_h = (qq[:, lo:lo + HEAD_PAD] * cosq + qq[:, wide + lo:wide + lo + HEAD_PAD] * sinq) * scale
        q_out[0, hd] = q_h.astype(BF16)
        k_out[0, hd] = (kn[:, lo:lo + HEAD_PAD] + k_rope).astype(BF16)
        v_out[0, hd] = (v[:, lo:lo + HEAD_PAD] + ones_lane).astype(BF16)


def _mla_weights(w_in_l, w_uq_l, w_ukv_l):
    half = QK_ROPE // 2
    pad_l, pad_r = QK_NOPE, HEAD_PAD - QK_NOPE - QK_ROPE
    o = Q_LORA + KV_LORA
    w_kr = w_in_l[:, o:o + QK_ROPE]
    w_kr_sw = jnp.concatenate([w_kr[:, half:], w_kr[:, :half]], axis=1)
    place = lambda w: jnp.pad(w, ((0, 0), (pad_l, pad_r)))
    win = jnp.concatenate([w_in_l[:, :o], place(w_kr), place(w_kr_sw)], axis=1).astype(BF16)
    wq3 = w_uq_l.reshape(Q_LORA, HEADS, QK_NOPE + QK_ROPE)
    wq_a = jnp.pad(wq3, ((0, 0), (0, 0), (0, pad_r))).reshape(Q_LORA, HEADS * HEAD_PAD)
    rope = wq3[..., QK_NOPE:]
    rope_sw = jnp.concatenate([rope[..., half:], rope[..., :half]], axis=-1)
    wq_b = jnp.pad(rope_sw, ((0, 0), (0, 0), (pad_l, pad_r))).reshape(Q_LORA, HEADS * HEAD_PAD)
    wq = jnp.concatenate([wq_a, wq_b], axis=1).astype(BF16)
    wkv3 = w_ukv_l.reshape(KV_LORA, HEADS, QK_NOPE + V_HEAD)
    wk = jnp.pad(wkv3[..., :QK_NOPE], ((0, 0), (0, 0), (0, HEAD_PAD - QK_NOPE))).reshape(KV_LORA, HEADS * HEAD_PAD)
    wv = jnp.pad(wkv3[..., QK_NOPE:], ((0, 0), (0, 0), (0, HEAD_PAD - V_HEAD))).reshape(KV_LORA, HEADS * HEAD_PAD)
    return win, wq, wk.astype(BF16), wv.astype(BF16)


def _mla_proj(h, win, qn, wq, kvn, wk, wv, cosq, sinq):
    b, s, d = h.shape
    ts = min(SEQ_TILE, s)
    tile = lambda w: pl.BlockSpec((1, ts, w), lambda i, j: (i, j, 0))
    head_out = lambda n: pl.BlockSpec((1, n, ts, HEAD_PAD), lambda i, j: (i, 0, j, 0))
    return pl.pallas_call(
        _mla_proj_kernel,
        grid=(b, s // ts),
        in_specs=[tile(d), _full(win.shape), _full((1, Q_LORA)), _full(wq.shape),
                  _full((1, KV_LORA)), _full(wk.shape), _full(wv.shape), tile(HEAD_PAD), tile(HEAD_PAD)],
        out_specs=[head_out(HEADS)] * 3,
        out_shape=[jax.ShapeDtypeStruct((b, HEADS, s, HEAD_PAD), BF16)] * 3,
        compiler_params=_params("arbitrary", "arbitrary"),
        name="mla_proj",
    )(h, win, qn.reshape(1, -1), wq, kvn.reshape(1, -1), wk, wv, cosq, sinq)


def _attn_kernel(q_ref, k_ref, v_ref, o_ref, sa_ref, sb_ref, m_ref, acc_ref, *, tq, nq):
    row = lax.broadcasted_iota(jnp.int32, (tq, tq), 0)
    col = lax.broadcasted_iota(jnp.int32, (tq, tq), 1)
    causal = row >= col
    lane = lax.broadcasted_iota(jnp.int32, (tq, LANES), 1)
    reps = tq // LANES

    def produce(dst_ref, hh, qi, kb):
        q0 = pl.multiple_of(qi * tq, tq)
        k0 = pl.multiple_of(kb * tq, tq)
        dst_ref[hh] = _dot_nt(q_ref[0, hh, pl.ds(q0, tq), :], k_ref[0, hh, pl.ds(k0, tq), :])

    def consume(src_ref, hh, kb, masked):
        k0 = pl.multiple_of(kb * tq, tq)
        sc = src_ref[hh]
        if masked:
            sc = jnp.where(causal, sc, NEG_BIG)
        m = m_ref[hh]
        m_new = jnp.maximum(m, jnp.max(sc, axis=-1, keepdims=True))
        p = jnp.exp2(sc - jnp.tile(m_new, (1, reps))).astype(BF16)
        acc_ref[hh] = jnp.exp2(m - m_new) * acc_ref[hh] + _dot(p, v_ref[0, hh, pl.ds(k0, tq), :])
        m_ref[hh] = m_new

    def step(src_ref, dst_ref, kb, masked, nqi, nkb):
        for hh in range(min(ATT_SKEW, ATT_HEADS)):
            produce(dst_ref, hh, nqi, nkb)
        for hh in range(ATT_HEADS):
            consume(src_ref, hh, kb, masked)
            if hh + ATT_SKEW < ATT_HEADS:
                produce(dst_ref, hh + ATT_SKEW, nqi, nkb)

    def reset():
        m_ref[...] = jnp.full(m_ref.shape, NEG_BIG, F32)
        acc_ref[...] = jnp.zeros(acc_ref.shape, F32)

    def finalize(qi):
        q0 = pl.multiple_of(qi * tq, tq)
        for pair in range(ATT_HEADS // 2):
            acc0 = acc_ref[2 * pair]
            acc1 = acc_ref[2 * pair + 1]
            o0 = acc0 / acc0[:, V_HEAD:V_HEAD + 1]
            o1 = acc1 / acc1[:, V_HEAD:V_HEAD + 1]
            o_ref[0, pl.ds(q0, tq), pair * LANES:(pair + 1) * LANES] = jnp.where(
                lane < V_HEAD, o0, pltpu.roll(o1, V_HEAD, 1)).astype(BF16)

    n_steps = nq * (nq + 1) // 2
    bufs = (sa_ref, sb_ref)
    reset()
    for hh in range(ATT_HEADS):
        produce(sa_ref, hh, 0, 0)

    def body(t, carry):
        qi, kb = carry
        last = kb == qi
        nqi = jnp.where(last, qi + 1, qi)
        nkb = jnp.where(last, 0, kb + 1)
        for par in range(2):
            src, dst = bufs[par], bufs[1 - par]

            @pl.when((t % 2 == par) & jnp.logical_not(last))
            def _():
                step(src, dst, kb, False, nqi, nkb)

            @pl.when((t % 2 == par) & last)
            def _():
                step(src, dst, kb, True, nqi, nkb)
                finalize(qi)
                reset()

        return nqi, nkb

    lax.fori_loop(0, n_steps - 1, body, (jnp.int32(0), jnp.int32(0)))
    for hh in range(ATT_HEADS):
        consume(bufs[(n_steps - 1) % 2], hh, nq - 1, True)
    finalize(nq - 1)


def _attention(q, k, v):
    b, _, s, _ = q.shape
    tq = min(ATT_TILE, s)
    spec = pl.BlockSpec((1, ATT_HEADS, s, HEAD_PAD), lambda i, hg: (i, hg, 0, 0))
    return pl.pallas_call(
        functools.partial(_attn_kernel, tq=tq, nq=s // tq),
        grid=(b, HEADS // ATT_HEADS),
        in_specs=[spec, spec, spec],
        out_specs=pl.BlockSpec((1, s, ATT_HEADS * V_HEAD), lambda i, hg: (i, 0, hg)),
        out_shape=jax.ShapeDtypeStruct((b, s, MIX), BF16),
        scratch_shapes=[pltpu.VMEM((ATT_HEADS, tq, tq), F32), pltpu.VMEM((ATT_HEADS, tq, tq), F32),
                        pltpu.VMEM((ATT_HEADS, tq, LANES), F32), pltpu.VMEM((ATT_HEADS, tq, HEAD_PAD), F32)],
        compiler_params=_params("arbitrary", "arbitrary"),
        name="mla_attention",
    )(q, k, v)


def _sub_tiles(seq_len, ts, want):
    n = seq_len // ts
    return max(k for k in range(1, want + 1) if n % k == 0)


def _run_ahead(sub, project, work):
    project(0)
    for k in range(sub):
        if k + 1 < sub:
            project(k + 1)
        work(k)


def _pool_kernel(h_ref, win_ref, wp_ref, ps_ref, o_ref, ext_ref, y_ref, *, ts, sub):
    si = pl.program_id(1)

    @pl.when(si == 0)
    def _():
        ext_ref[0:POOL_HALO, :] = jnp.zeros((POOL_HALO, MIX), F32)

    def project(k):
        r0 = POOL_HALO + k * ts
        ext_ref[r0:r0 + ts, :] = _dot(h_ref[0, k * ts:(k + 1) * ts, :], win_ref[...])

    def work(k):
        r0 = POOL_HALO + k * ts
        t = (si * sub + k) * ts + lax.broadcasted_iota(jnp.int32, (ts, 1), 0)
        for g, w in enumerate(POOL_WINDOWS):
            lo = g * POOL_GROUP
            u_g = ext_ref[r0:r0 + ts, lo:lo + POOL_GROUP]
            win = u_g
            for kk in range(1, w):
                win = win + ext_ref[r0 - kk:r0 - kk + ts, lo:lo + POOL_GROUP]
            count = jnp.minimum(t + 1, w).astype(F32)
            d = win / count - u_g
            y = _dot(d.astype(BF16), wp_ref[g])
            y_ref[k * ts:(k + 1) * ts, lo:lo + POOL_GROUP] = (y * ps_ref[:, lo:lo + POOL_GROUP]).astype(BF16)

    _run_ahead(sub, project, work)
    ext_ref[0:POOL_HALO, :] = ext_ref[sub * ts:sub * ts + POOL_HALO, :]
    o_ref[0] = y_ref[...]


def _pool(h, win, wp, ps):
    b, s, d = h.shape
    ts = min(SEQ_TILE, s)
    sub = _sub_tiles(s, ts, POOL_SUB)
    rows = sub * ts
    return pl.pallas_call(
        functools.partial(_pool_kernel, ts=ts, sub=sub),
        grid=(b, s // rows),
        in_specs=[pl.BlockSpec((1, rows, d), lambda i, j: (i, j, 0)), _full(win.shape),
                  _full(wp.shape), _full((1, MIX))],
        out_specs=pl.BlockSpec((1, rows, MIX), lambda i, j: (i, j, 0)),
        out_shape=jax.ShapeDtypeStruct((b, s, MIX), BF16),
        scratch_shapes=[pltpu.VMEM((POOL_HALO + rows, MIX), F32), pltpu.VMEM((rows, MIX), BF16)],
        compiler_params=_params("arbitrary", "arbitrary"),
        name="pool_mixer",
    )(h, win, wp, ps.reshape(1, MIX))


def _causal_conv(buf_ref, r0, c0, width, cw_ref, cb_ref, ts):
    def rows(back):
        return buf_ref[r0 - back:r0 - back + ts, c0:c0 + width]

    acc = rows(0) * cw_ref[CONV_WIDTH - 1:CONV_WIDTH, :] + cb_ref[...]
    for jj in range(CONV_WIDTH - 1):
        acc = acc + rows(CONV_WIDTH - 1 - jj) * cw_ref[jj:jj + 1, :]
    return acc


SSD_IN_COLS = MIX + SSD_XBC + LANES
SSD_GROUP_COLS = MIX // SSD_GROUPS
SSD_GROUP_HEADS = SSD_HEADS // SSD_GROUPS


def _ssd_kernel(h_ref, win_ref, cw_ref, cb_ref, dtb_ref, alog_ref, dskip_ref, ng_ref,
                tri_ref, expand_ref, bd_ref, o_ref, u_ref, xbc_ref, y_ref, state_ref, *, ts, sub):
    si = pl.program_id(1)
    lc = SSD_CHUNK
    row = lax.broadcasted_iota(jnp.int32, (lc, lc), 0)
    col = lax.broadcasted_iota(jnp.int32, (lc, lc), 1)
    tril = row >= col
    lane = lax.broadcasted_iota(jnp.int32, (1, LANES), 1)

    @pl.when(si == 0)
    def _():
        state_ref[...] = jnp.zeros(state_ref.shape, F32)
        u_ref[0:CONV_HALO, :] = jnp.zeros((CONV_HALO, SSD_IN_COLS), F32)

    def project(k):
        r0 = CONV_HALO + k * ts
        u_ref[r0:r0 + ts, :] = _dot(h_ref[0, k * ts:(k + 1) * ts, :], win_ref[...])

    def work(k):
        r0 = CONV_HALO + k * ts
        xbc_ref[k] = _silu(_causal_conv(u_ref, r0, MIX, SSD_XBC, cw_ref, cb_ref, ts))
        dt_all = _softplus(u_ref[r0:r0 + ts, MIX + SSD_XBC:] + dtb_ref[...])
        a_all = dt_all * (-jnp.exp(alog_ref[...]))
        ys = []
        for c in range(ts // lc):
            c0 = c * lc
            xs = xbc_ref[k, c0:c0 + lc, 0:MIX]
            bm = xbc_ref[k, c0:c0 + lc, MIX:MIX + LANES]
            cm = xbc_ref[k, c0:c0 + lc, MIX + LANES:MIX + 2 * LANES]
            dt = dt_all[c0:c0 + lc]
            cs = _sel_dot_left(tri_ref[...], a_all[c0:c0 + lc])
            cs_t = cs.T
            dt_t = dt.T
            cs_end = cs[lc - 1:lc, :]
            per_head = jnp.concatenate([jnp.exp(cs), dt * jnp.exp(cs_end - cs)], axis=0)
            wide = _sel_dot2(per_head, expand_ref[...])
            ecs_x, dd_x = wide[0:lc], wide[lc:2 * lc]
            xs_b = xs.astype(BF16)
            xdd = (xs * dd_x).astype(BF16)
            y_parts = []
            for grp in range(SSD_GROUPS):
                in_grp = (lane >= grp * SSD_STATE) & (lane < (grp + 1) * SSD_STATE)
                cm_g = jnp.where(in_grp, cm, 0.0).astype(BF16)
                bm_g = jnp.where(in_grp, bm, 0.0).astype(BF16)
                cb = _dot_nt(cm_g, bm_g)
                gc = grp * SSD_GROUP_COLS
                masks = []
                for hl in range(SSD_GROUP_HEADS):
                    hd = grp * SSD_GROUP_HEADS + hl
                    seg = jnp.where(tril, cs[:, hd:hd + 1] - cs_t[hd:hd + 1, :], NEG_BIG)
                    masks.append((cb * jnp.exp(seg) * dt_t[hd:hd + 1, :]).astype(BF16))
                m_cat = jnp.concatenate(masks, axis=1)
                x_bd = jnp.tile(xs_b[:, gc:gc + SSD_GROUP_COLS], (SSD_GROUP_HEADS, 1)) * bd_ref[...]
                y_diag = _dot(m_cat, x_bd)
                st = state_ref[grp]
                y_off = _dot(cm_g, st.astype(BF16)) * ecs_x[:, gc:gc + SSD_GROUP_COLS]
                state_ref[grp] = (st * ecs_x[lc - 1:lc, gc:gc + SSD_GROUP_COLS]
                                  + _dot_tn(bm_g, xdd[:, gc:gc + SSD_GROUP_COLS]))
                y_parts.append(y_diag + y_off)
            ys.append(jnp.concatenate(y_parts, axis=1) + xs * dskip_ref[...])
        y = jnp.concatenate(ys, axis=0) * _silu(u_ref[r0:r0 + ts, 0:MIX])
        y_ref[k * ts:(k + 1) * ts, :] = _rmsnorm(y, ng_ref[...]).astype(BF16)

    _run_ahead(sub, project, work)
    u_ref[0:CONV_HALO, :] = u_ref[sub * ts:sub * ts + CONV_HALO, :]
    o_ref[0] = y_ref[...]


def _sel_dot_left(sel, a):
    hi, mid, lo = _split3(a)
    return _dot(sel, hi) + _dot(sel, mid) + _dot(sel, lo)


def _ssd_constants():
    lc = SSD_CHUNK
    tri = (jnp.arange(lc)[:, None] >= jnp.arange(lc)[None, :]).astype(BF16)
    expand = (jnp.arange(LANES)[:, None] == (jnp.arange(MIX)[None, :] // SSD_HEADDIM)).astype(BF16)
    rows = jnp.arange(SSD_GROUP_HEADS * lc)[:, None] // lc
    cols = jnp.arange(SSD_GROUP_COLS)[None, :] // SSD_HEADDIM
    bd = (rows == cols).astype(BF16)
    return tri, expand, bd


def _ssd(h, win, cw, cb, dtb, alog, dskip, ng):
    b, s, d = h.shape
    ts = min(SEQ_TILE, s)
    sub = _sub_tiles(s, ts, SSD_SUB)
    rows = sub * ts
    tri, expand, bd = _ssd_constants()
    pad = lambda v: jnp.pad(v, (0, LANES - SSD_HEADS)).reshape(1, LANES)
    return pl.pallas_call(
        functools.partial(_ssd_kernel, ts=ts, sub=sub),
        grid=(b, s // rows),
        in_specs=[pl.BlockSpec((1, rows, d), lambda i, j: (i, j, 0)), _full(win.shape),
                  _full((CONV_WIDTH, SSD_XBC)), _full((1, SSD_XBC)), _full((1, LANES)), _full((1, LANES)),
                  _full((1, MIX)), _full((1, MIX)), _full(tri.shape), _full(expand.shape), _full(bd.shape)],
        out_specs=pl.BlockSpec((1, rows, MIX), lambda i, j: (i, j, 0)),
        out_shape=jax.ShapeDtypeStruct((b, s, MIX), BF16),
        scratch_shapes=[pltpu.VMEM((CONV_HALO + rows, SSD_IN_COLS), F32),
                        pltpu.VMEM((sub, ts, SSD_XBC), F32),
                        pltpu.VMEM((rows, MIX), BF16),
                        pltpu.VMEM((SSD_GROUPS, LANES, SSD_GROUP_COLS), F32)],
        compiler_params=_params("arbitrary", "arbitrary"),
        name="ssd_mixer",
    )(h, win, cw, cb.reshape(1, -1), pad(dtb), pad(alog),
      jnp.repeat(dskip, SSD_HEADDIM).reshape(1, MIX), ng.reshape(1, MIX), tri, expand, bd)


def _lru_kernel(hn_ref, win_ref, cw_ref, cb_ref, wai_ref, bai_ref, lam_ref, o_ref,
                u_ref, a_ref, h_ref, y_ref, carry_ref, *, ts, sub):
    si = pl.program_id(1)
    rowi = lax.broadcasted_iota(jnp.int32, (SUBLANES, MIX), 0)

    @pl.when(si == 0)
    def _():
        carry_ref[...] = jnp.zeros(carry_ref.shape, F32)
        u_ref[0:CONV_HALO, :] = jnp.zeros((CONV_HALO, 2 * MIX), F32)

    def project(k):
        r0 = CONV_HALO + k * ts
        u_ref[r0:r0 + ts, :] = _dot(hn_ref[0, k * ts:(k + 1) * ts, :], win_ref[...])

    def work(k):
        r0 = CONV_HALO + k * ts
        xc = _causal_conv(u_ref, r0, MIX, MIX, cw_ref, cb_ref, ts)
        ri = _dot(xc.astype(BF16), wai_ref[...]) + bai_ref[...]
        r_t = _sigmoid(ri[:, :MIX])
        i_t = _sigmoid(ri[:, MIX:])
        log_a = (-LRU_C) * r_t * _softplus(-lam_ref[...])
        a_t = jnp.exp(log_a)
        mult = jnp.sqrt(jnp.tanh(-log_a) * (a_t * a_t + 1.0))
        a_ref[k] = a_t
        h_ref[k] = xc * i_t * mult
        carry = carry_ref[...]
        for blk in range(ts // SUBLANES):
            b0 = blk * SUBLANES
            a8 = a_ref[k, b0:b0 + SUBLANES, :]
            u8 = h_ref[k, b0:b0 + SUBLANES, :]
            for kk in (1, 2, 4):
                keep = rowi >= kk
                u8 = jnp.where(keep, a8 * pltpu.roll(u8, kk, 0) + u8, u8)
                a8 = jnp.where(keep, a8 * pltpu.roll(a8, kk, 0), a8)
            h8 = u8 + a8 * carry
            h_ref[k, b0:b0 + SUBLANES, :] = h8
            carry = h8[SUBLANES - 1:SUBLANES, :]
        carry_ref[...] = carry
        y_ref[k * ts:(k + 1) * ts, :] = (h_ref[k] * _gelu_tanh(u_ref[r0:r0 + ts, 0:MIX])).astype(BF16)

    _run_ahead(sub, project, work)
    u_ref[0:CONV_HALO, :] = u_ref[sub * ts:sub * ts + CONV_HALO, :]
    o_ref[0] = y_ref[...]


def _block_diag(w):
    nb, n, _ = w.shape
    eye = jnp.eye(nb, dtype=w.dtype)
    return (eye[:, None, :, None] * w[:, :, None, :]).reshape(nb * n, nb * n)


def _lru(h, win, cw, cb, wai, bai, lam):
    b, s, d = h.shape
    ts = min(SEQ_TILE, s)
    sub = _sub_tiles(s, ts, LRU_SUB)
    rows = sub * ts
    return pl.pallas_call(
        functools.partial(_lru_kernel, ts=ts, sub=sub),
        grid=(b, s // rows),
        in_specs=[pl.BlockSpec((1, rows, d), lambda i, j: (i, j, 0)), _full(win.shape),
                  _full((CONV_WIDTH, MIX)), _full((1, MIX)), _full(wai.shape), _full((1, 2 * MIX)), _full((1, MIX))],
        out_specs=pl.BlockSpec((1, rows, MIX), lambda i, j: (i, j, 0)),
        out_shape=jax.ShapeDtypeStruct((b, s, MIX), BF16),
        scratch_shapes=[pltpu.VMEM((CONV_HALO + rows, 2 * MIX), F32), pltpu.VMEM((sub, ts, MIX), F32),
                        pltpu.VMEM((sub, ts, MIX), F32), pltpu.VMEM((rows, MIX), BF16), pltpu.VMEM((1, MIX), F32)],
        compiler_params=_params("arbitrary", "arbitrary"),
        name="rglru_mixer",
    )(h, win, cw, cb.reshape(1, MIX), wai, bai.reshape(1, 2 * MIX), lam.reshape(1, MIX))


def _merge_kernel(x_ref, h_ref, wg_ref, ya_ref, yb_ref, yc_ref, yd_ref, wb_ref, wo_ref, o_ref):
    h = h_ref[...]
    merged = None
    for n, y_ref in enumerate((ya_ref, yb_ref, yc_ref, yd_ref)):
        gate = _sigmoid(_dot(h, wg_ref[:, n * D_MODEL:(n + 1) * D_MODEL]))
        term = gate * _dot(y_ref[...], wb_ref[n])
        merged = term if merged is None else merged + term
    o_ref[...] = x_ref[...] + _dot(merged.astype(BF16), wo_ref[...])


def _merge(x2, h2, wg, ys, wb, wo):
    t, d = x2.shape
    tm = min(TOK_TILE, t)
    row = lambda w: pl.BlockSpec((tm, w), lambda i: (i, 0))
    return pl.pallas_call(
        _merge_kernel,
        grid=(t // tm,),
        in_specs=[row(d), row(d), _full(wg.shape), row(MIX), row(MIX), row(MIX), row(MIX),
                  _full(wb.shape), _full(wo.shape)],
        out_specs=row(d),
        out_shape=jax.ShapeDtypeStruct((t, d), F32),
        compiler_params=_params("arbitrary"),
        name="gated_merge",
    )(x2, h2, wg, *ys, wb, wo)


FF_CHUNK = 1024


def _mlp_ple_kernel(x_ref, g_ref, w1_ref, w2_ref, gp_ref, wpg_ref, p_ref, wple_ref, gn_ref, *out_refs, last):
    x = x_ref[...]
    h = _rmsnorm(x, g_ref[...]).astype(BF16)
    acc = x
    for c in range(D_FF // FF_CHUNK):
        hid = jnp.maximum(_dot(h, w1_ref[:, c * FF_CHUNK:(c + 1) * FF_CHUNK]), 0.0)
        acc = acc + _dot((hid * hid).astype(BF16), w2_ref[c * FF_CHUNK:(c + 1) * FF_CHUNK, :])
    gate = _sigmoid(_dot(_rmsnorm(acc, gp_ref[...]).astype(BF16), wpg_ref[...]))
    y = acc + _dot(p_ref[0].astype(BF16), wple_ref[...]) * gate
    normed = _rmsnorm(y, gn_ref[...])
    if last:
        out_refs[0][...] = normed
    else:
        out_refs[0][...] = y
        out_refs[1][...] = normed.astype(BF16)


def _mlp_ple(x2, g, w1, w2, gp, wpg, p3, layer, wple, g_next, last):
    t, d = x2.shape
    tm = min(TOK_TILE, t)
    row = lambda w: pl.BlockSpec((tm, w), lambda i: (i, 0))
    out_specs = [row(d)] if last else [row(d), row(d)]
    out_shape = [jax.ShapeDtypeStruct((t, d), F32)] + ([] if last else [jax.ShapeDtypeStruct((t, d), BF16)])
    return pl.pallas_call(
        functools.partial(_mlp_ple_kernel, last=last),
        grid=(t // tm,),
        in_specs=[row(d), _full((1, d)), _resident(w1.shape), _resident(w2.shape), _full((1, d)),
                  _resident(wpg.shape), pl.BlockSpec((1, tm, PLE_DIM), lambda i: (layer, i, 0)),
                  _resident(wple.shape), _full((1, d))],
        out_specs=out_specs,
        out_shape=out_shape,
        compiler_params=_params("arbitrary"),
        name="mlp_ple",
    )(x2, g.reshape(1, d), w1, w2, gp.reshape(1, d), wpg, p3, wple, g_next.reshape(1, d))


_SPLIT = (Q_LORA, KV_LORA, QK_ROPE, MIX, MIX, SSD_XBC, SSD_HEADS, MIX, MIX, N_BRANCH * D_MODEL)


def _col_offsets():
    offs, acc = [], 0
    for sz in _SPLIT:
        offs.append(acc)
        acc += sz
    return offs


def kernel(x, p, positions, g_mix, w_in, q_norm, w_uq, kv_norm, w_ukv, w_pool, pool_scale,
           ssd_conv_w, ssd_conv_b, ssd_dt_bias, ssd_a_log, ssd_d, ssd_norm,
           lru_conv_w, lru_conv_b, lru_w_a, lru_b_a, lru_w_i, lru_b_i, lru_lambda,
           w_branch, w_out, g_mlp, w_ff1, w_ff2, g_ple, w_ple_gate, w_ple, g_final):
    b, s, d = x.shape
    depth = w_in.shape[0]
    t = b * s
    offs = _col_offsets()
    o_pool, o_z, o_xbc, o_dt, o_lg, o_lx, o_gate = offs[3], offs[4], offs[5], offs[6], offs[7], offs[8], offs[9]
    cosq, sinq = _rope_tables(positions)
    p3 = p.reshape(depth, t, PLE_DIM)
    x2 = x.reshape(t, d)
    h2 = _prenorm(x2, g_mix[0])
    for l in range(depth):
        wl = w_in[l]
        h = h2.reshape(b, s, d)
        win_a, wq, wk, wv = _mla_weights(wl, w_uq[l], w_ukv[l])
        q, k, v = _mla_proj(h, win_a, q_norm[l], wq, kv_norm[l], wk, wv, cosq, sinq)
        y_a = _attention(q, k, v)
        y_b = _pool(h, wl[:, o_pool:o_pool + MIX].astype(BF16), w_pool[l].astype(BF16), pool_scale[l])
        w_dt = jnp.pad(wl[:, o_dt:o_dt + SSD_HEADS], ((0, 0), (0, LANES - SSD_HEADS)))
        win_c = jnp.concatenate([wl[:, o_z:o_z + MIX], wl[:, o_xbc:o_xbc + SSD_XBC], w_dt], axis=1).astype(BF16)
        y_c = _ssd(h, win_c, ssd_conv_w[l], ssd_conv_b[l], ssd_dt_bias[l], ssd_a_log[l], ssd_d[l], ssd_norm[l])
        win_d = wl[:, o_lg:o_lg + 2 * MIX].astype(BF16)
        wai = jnp.concatenate([_block_diag(lru_w_a[l]), _block_diag(lru_w_i[l])], axis=1).astype(BF16)
        bai = jnp.concatenate([lru_b_a[l], lru_b_i[l]])
        y_d = _lru(h, win_d, lru_conv_w[l], lru_conv_b[l], wai, bai, lru_lambda[l])
        ys = [y.reshape(t, MIX) for y in (y_a, y_b, y_c, y_d)]
        x2 = _merge(x2, h2, wl[:, o_gate:].astype(BF16), ys, w_branch[l].astype(BF16), w_out[l].astype(BF16))
        last = l == depth - 1
        outs = _mlp_ple(x2, g_mlp[l], w_ff1[l].astype(BF16), w_ff2[l].astype(BF16), g_ple[l],
                        w_ple_gate[l].astype(BF16), p3, l, w_ple[l].astype(BF16),
                        g_final if last else g_mix[l + 1], last)
        if last:
            x2 = outs[0]
        else:
            x2, h2 = outs
    return x2.reshape(b, s, d)
```

```python
import functools

import jax
import jax.numpy as jnp
import numpy as np
from jax import lax
from jax.experimental import pallas as pl
from jax.experimental.pallas import tpu as pltpu

F32 = jnp.float32
BF16 = jnp.bfloat16

D_MODEL = 1024
MIX = 512
N_BRANCH = 4
HEADS = 8
QK_NOPE = 64
QK_ROPE = 32
V_HEAD = 64
Q_LORA = 384
KV_LORA = 256
ROPE_THETA = 10000.0
POOL_WINDOWS = (2, 4, 8, 16)
POOL_GROUP = 128
POOL_HALO = 16
SSD_HEADS = 8
SSD_HEADDIM = 64
SSD_GROUPS = 2
SSD_STATE = 64
SSD_CHUNK = 128
CONV_WIDTH = 4
CONV_HALO = 8
SSD_XBC = 768
LRU_BLOCKS = 8
LRU_BLOCK = 64
LRU_C = 8.0
D_FF = 4096
PLE_DIM = 256
EPS = 1e-6
LANES = 128
SUBLANES = 8
HEAD_PAD = 128
NEG_BIG = -1e30
LOG2_E = 1.4426950408889634
VMEM_LIMIT = 56 * 1024 * 1024

SEQ_TILE = 512
POOL_SUB = 4
SSD_SUB = 2
LRU_SUB = 2
NORM_TILE = 2048
TOK_TILE = 1024
ATT_TILE = 512
ATT_HEADS = 4
ATT_SKEW = 1


def _dot(a, b):
    return jnp.dot(a, b, preferred_element_type=F32)


def _dot_nt(a, b):
    return lax.dot_general(a, b, (((1,), (1,)), ((), ())), preferred_element_type=F32)


def _dot_tn(a, b):
    return lax.dot_general(a, b, (((0,), (0,)), ((), ())), preferred_element_type=F32)


def _split3(a):
    hi = a.astype(BF16)
    r1 = a - hi.astype(F32)
    mid = r1.astype(BF16)
    lo = (r1 - mid.astype(F32)).astype(BF16)
    return hi, mid, lo


def _sel_dot2(a, sel):
    hi = a.astype(BF16)
    mid = (a - hi.astype(F32)).astype(BF16)
    return _dot(hi, sel) + _dot(mid, sel)


def _rmsnorm(x, g):
    return x * lax.rsqrt(jnp.mean(x * x, axis=-1, keepdims=True) + EPS) * g


def _sigmoid(x):
    return jax.nn.sigmoid(x)


def _silu(x):
    return x * jax.nn.sigmoid(x)


def _softplus(x):
    return jnp.maximum(x, 0.0) + jnp.log1p(jnp.exp(-jnp.abs(x)))


def _gelu_tanh(x):
    return 0.5 * x * (1.0 + jnp.tanh(0.7978845608028654 * (x + 0.044715 * (x * x * x))))


def _params(*sem):
    return pltpu.CompilerParams(dimension_semantics=sem, vmem_limit_bytes=VMEM_LIMIT)


def _full(shape):
    n = len(shape)
    return pl.BlockSpec(shape, lambda *_: (0,) * n)


def _layer(stack, l, resident=False):
    n = stack.ndim - 1
    mode = dict(pipeline_mode=pl.Buffered(1)) if resident else {}
    return pl.BlockSpec((None,) + tuple(stack.shape[1:]), lambda *_: (l,) + (0,) * n, **mode)


def _prenorm_kernel(x_ref, g_ref, h_ref):
    h_ref[...] = _rmsnorm(x_ref[...], g_ref[...]).astype(BF16)


def _prenorm(x2, l, g):
    t, d = x2.shape
    tm = min(NORM_TILE, t)
    row = pl.BlockSpec((tm, d), lambda i: (i, 0))
    return pl.pallas_call(
        _prenorm_kernel,
        grid=(t // tm,),
        in_specs=[row, _layer(g, l)],
        out_specs=row,
        out_shape=jax.ShapeDtypeStruct((t, d), BF16),
        compiler_params=_params("arbitrary"),
        name="prenorm",
    )(x2, g)


ROPE_HALF = QK_ROPE // 2
ROPE_PACK = LANES // ROPE_HALF


def _rope_kernel(ang_ref, place_ref, cos_ref, sin_ref, *, blk):
    a = ang_ref[...]
    cos_parts = _split3(jnp.cos(a))
    sin_parts = _split3(jnp.sin(a))
    lane = lax.broadcasted_iota(jnp.int32, (1, LANES), 1)
    ones = (lane < QK_NOPE).astype(F32)
    for i in range(ROPE_PACK):
        ci = sum(_dot(part, place_ref[0, i]) for part in cos_parts) + ones
        si = sum(_dot(part, place_ref[1, i]) for part in sin_parts)
        cos_ref[pl.ds(i, blk, stride=ROPE_PACK), :] = ci
        sin_ref[pl.ds(i, blk, stride=ROPE_PACK), :] = si


def _rope_placement():
    place = np.zeros((2, ROPE_PACK, LANES, LANES), np.float32)
    for i in range(ROPE_PACK):
        for j in range(ROPE_HALF):
            src = i * ROPE_HALF + j
            place[0, i, src, QK_NOPE + j] = 1.0
            place[0, i, src, QK_NOPE + ROPE_HALF + j] = 1.0
            place[1, i, src, QK_NOPE + j] = -1.0
            place[1, i, src, QK_NOPE + ROPE_HALF + j] = 1.0
    return jnp.asarray(place, BF16)


def _rope_tables(positions):
    b, s = positions.shape
    t = b * s
    inv = 1.0 / (ROPE_THETA ** (jnp.arange(0, QK_ROPE, 2, dtype=F32) / QK_ROPE))
    rows = t // ROPE_PACK
    pos = positions.astype(F32).reshape(rows, ROPE_PACK)
    ang = jnp.repeat(pos, ROPE_HALF, axis=1) * jnp.tile(inv, ROPE_PACK)[None, :]
    blk = min(rows, 512)
    place = _rope_placement()
    out_spec = pl.BlockSpec((blk * ROPE_PACK, LANES), lambda i: (i, 0))
    cosq, sinq = pl.pallas_call(
        functools.partial(_rope_kernel, blk=blk),
        grid=(rows // blk,),
        in_specs=[pl.BlockSpec((blk, LANES), lambda i: (i, 0)), _full(place.shape)],
        out_specs=[out_spec, out_spec],
        out_shape=[jax.ShapeDtypeStruct((t, LANES), F32)] * 2,
        compiler_params=_params("arbitrary"),
        name="rope_tables",
    )(ang, place)
    return cosq.reshape(b, s, LANES), sinq.reshape(b, s, LANES)


MLA_IN_COLS = Q_LORA + KV_LORA + 2 * HEAD_PAD


def _mla_proj_kernel(h_ref, win_ref, qn_ref, wq_ref, kvn_ref, wk_ref, wv_ref, cos_ref, sin_ref,
                     q_out, k_out, v_out):
    scale = (QK_NOPE + QK_ROPE) ** -0.5 * LOG2_E
    u = _dot(h_ref[0], win_ref[...])
    c_q = u[:, :Q_LORA]
    c_kv = u[:, Q_LORA:Q_LORA + KV_LORA]
    kr = u[:, Q_LORA + KV_LORA:Q_LORA + KV_LORA + HEAD_PAD]
    kr_sw = u[:, Q_LORA + KV_LORA + HEAD_PAD:]
    cosq = cos_ref[0]
    sinq = sin_ref[0]
    cqn = _rmsnorm(c_q, qn_ref[...]).astype(BF16)
    qq = _dot(cqn, wq_ref[...])
    ckvn = _rmsnorm(c_kv, kvn_ref[...]).astype(BF16)
    kn = _dot(ckvn, wk_ref[...])
    v = _dot(ckvn, wv_ref[...])
    lane = lax.broadcasted_iota(jnp.int32, (1, HEAD_PAD), 1)
    ones_lane = (lane == V_HEAD).astype(F32)
    k_rope = kr * cosq + kr_sw * sinq
    wide = HEADS * HEAD_PAD
    for hd in range(HEADS):
        lo = hd * HEAD_PAD
        q_h = (qq[:, lo:lo + HEAD_PAD] * cosq + qq[:, wide + lo:wide + lo + HEAD_PAD] * sinq) * scale
        q_out[0, hd] = q_h.astype(BF16)
        k_out[0, hd] = (kn[:, lo:lo + HEAD_PAD] + k_rope).astype(BF16)
        v_out[0, hd] = (v[:, lo:lo + HEAD_PAD] + ones_lane).astype(BF16)


def _mla_weights(w_in, w_uq, w_ukv):
    depth = w_in.shape[0]
    half = QK_ROPE // 2
    pad_l, pad_r = QK_NOPE, HEAD_PAD - QK_NOPE - QK_ROPE
    o = Q_LORA + KV_LORA
    w_kr = w_in[..., o:o + QK_ROPE]
    w_kr_sw = jnp.concatenate([w_kr[..., half:], w_kr[..., :half]], axis=-1)
    place = lambda w: jnp.pad(w, ((0, 0), (0, 0), (pad_l, pad_r)))
    win = jnp.concatenate([w_in[..., :o], place(w_kr), place(w_kr_sw)], axis=-1).astype(BF16)
    wq4 = w_uq.reshape(depth, Q_LORA, HEADS, QK_NOPE + QK_ROPE)
    wq_a = jnp.pad(wq4, ((0, 0), (0, 0), (0, 0), (0, pad_r))).reshape(depth, Q_LORA, HEADS * HEAD_PAD)
    rope = wq4[..., QK_NOPE:]
    rope_sw = jnp.concatenate([rope[..., half:], rope[..., :half]], axis=-1)
    wq_b = jnp.pad(rope_sw, ((0, 0), (0, 0), (0, 0), (pad_l, pad_r))).reshape(depth, Q_LORA, HEADS * HEAD_PAD)
    wq = jnp.concatenate([wq_a, wq_b], axis=-1).astype(BF16)
    wkv4 = w_ukv.reshape(depth, KV_LORA, HEADS, QK_NOPE + V_HEAD)
    pad_head = lambda w, n: jnp.pad(w, ((0, 0), (0, 0), (0, 0), (0, HEAD_PAD - n))).reshape(
        depth, KV_LORA, HEADS * HEAD_PAD).astype(BF16)
    return win, wq, pad_head(wkv4[..., :QK_NOPE], QK_NOPE), pad_head(wkv4[..., QK_NOPE:], V_HEAD)


def _mla_proj(h, l, win, qn, wq, kvn, wk, wv, cosq, sinq):
    b, s, d = h.shape
    ts = min(SEQ_TILE, s)
    tile = lambda w: pl.BlockSpec((1, ts, w), lambda i, j: (i, j, 0))
    head_out = lambda n: pl.BlockSpec((1, n, ts, HEAD_PAD), lambda i, j: (i, 0, j, 0))
    return pl.pallas_call(
        _mla_proj_kernel,
        grid=(b, s // ts),
        in_specs=[tile(d), _layer(win, l), _layer(qn, l), _layer(wq, l), _layer(kvn, l), _layer(wk, l),
                  _layer(wv, l), tile(HEAD_PAD), tile(HEAD_PAD)],
        out_specs=[head_out(HEADS)] * 3,
        out_shape=[jax.ShapeDtypeStruct((b, HEADS, s, HEAD_PAD), BF16)] * 3,
        compiler_params=_params("arbitrary", "arbitrary"),
        name="mla_proj",
    )(h, win, qn, wq, kvn, wk, wv, cosq, sinq)


def _attn_kernel(q_ref, k_ref, v_ref, o_ref, sa_ref, sb_ref, m_ref, acc_ref, *, tq, nq):
    row = lax.broadcasted_iota(jnp.int32, (tq, tq), 0)
    col = lax.broadcasted_iota(jnp.int32, (tq, tq), 1)
    causal = row >= col
    lane = lax.broadcasted_iota(jnp.int32, (tq, LANES), 1)
    reps = tq // LANES

    def produce(dst_ref, hh, qi, kb):
        q0 = pl.multiple_of(qi * tq, tq)
        k0 = pl.multiple_of(kb * tq, tq)
        dst_ref[hh] = _dot_nt(q_ref[0, hh, pl.ds(q0, tq), :], k_ref[0, hh, pl.ds(k0, tq), :])

    def consume(src_ref, hh, kb, masked):
        k0 = pl.multiple_of(kb * tq, tq)
        sc = src_ref[hh]
        if masked:
            sc = jnp.where(causal, sc, NEG_BIG)
        m = m_ref[hh]
        m_new = jnp.maximum(m, jnp.max(sc, axis=-1, keepdims=True))
        p = jnp.exp2(sc - jnp.tile(m_new, (1, reps))).astype(BF16)
        acc_ref[hh] = jnp.exp2(m - m_new) * acc_ref[hh] + _dot(p, v_ref[0, hh, pl.ds(k0, tq), :])
        m_ref[hh] = m_new

    def step(src_ref, dst_ref, kb, masked, nqi, nkb):
        for hh in range(min(ATT_SKEW, ATT_HEADS)):
            produce(dst_ref, hh, nqi, nkb)
        for hh in range(ATT_HEADS):
            consume(src_ref, hh, kb, masked)
            if hh + ATT_SKEW < ATT_HEADS:
                produce(dst_ref, hh + ATT_SKEW, nqi, nkb)

    def reset():
        m_ref[...] = jnp.full(m_ref.shape, NEG_BIG, F32)
        acc_ref[...] = jnp.zeros(acc_ref.shape, F32)

    def finalize(qi):
        q0 = pl.multiple_of(qi * tq, tq)
        for pair in range(ATT_HEADS // 2):
            acc0 = acc_ref[2 * pair]
            acc1 = acc_ref[2 * pair + 1]
            o0 = acc0 / acc0[:, V_HEAD:V_HEAD + 1]
            o1 = acc1 / acc1[:, V_HEAD:V_HEAD + 1]
            o_ref[0, pl.ds(q0, tq), pair * LANES:(pair + 1) * LANES] = jnp.where(
                lane < V_HEAD, o0, pltpu.roll(o1, V_HEAD, 1)).astype(BF16)

    n_steps = nq * (nq + 1) // 2
    bufs = (sa_ref, sb_ref)
    reset()
    for hh in range(ATT_HEADS):
        produce(sa_ref, hh, 0, 0)

    def body(t, carry):
        qi, kb = carry
        last = kb == qi
        nqi = jnp.where(last, qi + 1, qi)
        nkb = jnp.where(last, 0, kb + 1)
        for par in range(2):
            src, dst = bufs[par], bufs[1 - par]

            @pl.when((t % 2 == par) & jnp.logical_not(last))
            def _():
                step(src, dst, kb, False, nqi, nkb)

            @pl.when((t % 2 == par) & last)
            def _():
                step(src, dst, kb, True, nqi, nkb)
                finalize(qi)
                reset()

        return nqi, nkb

    lax.fori_loop(0, n_steps - 1, body, (jnp.int32(0), jnp.int32(0)))
    for hh in range(ATT_HEADS):
        consume(bufs[(n_steps - 1) % 2], hh, nq - 1, True)
    finalize(nq - 1)


def _attention(q, k, v):
    b, _, s, _ = q.shape
    tq = min(ATT_TILE, s)
    spec = pl.BlockSpec((1, ATT_HEADS, s, HEAD_PAD), lambda i, hg: (i, hg, 0, 0))
    return pl.pallas_call(
        functools.partial(_attn_kernel, tq=tq, nq=s // tq),
        grid=(b, HEADS // ATT_HEADS),
        in_specs=[spec, spec, spec],
        out_specs=pl.BlockSpec((1, s, ATT_HEADS * V_HEAD), lambda i, hg: (i, 0, hg)),
        out_shape=jax.ShapeDtypeStruct((b, s, MIX), BF16),
        scratch_shapes=[pltpu.VMEM((ATT_HEADS, tq, tq), F32), pltpu.VMEM((ATT_HEADS, tq, tq), F32),
                        pltpu.VMEM((ATT_HEADS, tq, LANES), F32), pltpu.VMEM((ATT_HEADS, tq, HEAD_PAD), F32)],
        compiler_params=_params("arbitrary", "arbitrary"),
        name="mla_attention",
    )(q, k, v)


def _sub_tiles(seq_len, ts, want):
    n = seq_len // ts
    return max(k for k in range(1, want + 1) if n % k == 0)


def _run_ahead(sub, project, work):
    project(0)
    for k in range(sub):
        if k + 1 < sub:
            project(k + 1)
        work(k)


def _pool_kernel(h_ref, win_ref, wp_ref, ps_ref, o_ref, ext_ref, y_ref, *, ts, sub):
    si = pl.program_id(1)

    @pl.when(si == 0)
    def _():
        ext_ref[0:POOL_HALO, :] = jnp.zeros((POOL_HALO, MIX), F32)

    def project(k):
        r0 = POOL_HALO + k * ts
        ext_ref[r0:r0 + ts, :] = _dot(h_ref[0, k * ts:(k + 1) * ts, :], win_ref[...])

    def work(k):
        r0 = POOL_HALO + k * ts
        t = (si * sub + k) * ts + lax.broadcasted_iota(jnp.int32, (ts, 1), 0)
        for g, w in enumerate(POOL_WINDOWS):
            lo = g * POOL_GROUP
            u_g = ext_ref[r0:r0 + ts, lo:lo + POOL_GROUP]
            win = u_g
            for kk in range(1, w):
                win = win + ext_ref[r0 - kk:r0 - kk + ts, lo:lo + POOL_GROUP]
            count = jnp.minimum(t + 1, w).astype(F32)
            d = win / count - u_g
            y = _dot(d.astype(BF16), wp_ref[g])
            y_ref[k * ts:(k + 1) * ts, lo:lo + POOL_GROUP] = (y * ps_ref[:, lo:lo + POOL_GROUP]).astype(BF16)

    _run_ahead(sub, project, work)
    ext_ref[0:POOL_HALO, :] = ext_ref[sub * ts:sub * ts + POOL_HALO, :]
    o_ref[0] = y_ref[...]


def _pool(h, l, win, wp, ps):
    b, s, d = h.shape
    ts = min(SEQ_TILE, s)
    sub = _sub_tiles(s, ts, POOL_SUB)
    rows = sub * ts
    return pl.pallas_call(
        functools.partial(_pool_kernel, ts=ts, sub=sub),
        grid=(b, s // rows),
        in_specs=[pl.BlockSpec((1, rows, d), lambda i, j: (i, j, 0)), _layer(win, l), _layer(wp, l), _layer(ps, l)],
        out_specs=pl.BlockSpec((1, rows, MIX), lambda i, j: (i, j, 0)),
        out_shape=jax.ShapeDtypeStruct((b, s, MIX), BF16),
        scratch_shapes=[pltpu.VMEM((POOL_HALO + rows, MIX), F32), pltpu.VMEM((rows, MIX), BF16)],
        compiler_params=_params("arbitrary", "arbitrary"),
        name="pool_mixer",
    )(h, win, wp, ps)


def _causal_conv(buf_ref, r0, c0, width, cw_ref, cb_ref, ts):
    def rows(back):
        return buf_ref[r0 - back:r0 - back + ts, c0:c0 + width]

    acc = rows(0) * cw_ref[CONV_WIDTH - 1:CONV_WIDTH, :] + cb_ref[...]
    for jj in range(CONV_WIDTH - 1):
        acc = acc + rows(CONV_WIDTH - 1 - jj) * cw_ref[jj:jj + 1, :]
    return acc


SSD_IN_COLS = MIX + SSD_XBC + LANES
SSD_GROUP_COLS = MIX // SSD_GROUPS
SSD_GROUP_HEADS = SSD_HEADS // SSD_GROUPS


def _ssd_kernel(h_ref, win_ref, cw_ref, cb_ref, dtb_ref, alog_ref, dskip_ref, ng_ref,
                tri_ref, expand_ref, bd_ref, o_ref, u_ref, xbc_ref, y_ref, state_ref, *, ts, sub):
    si = pl.program_id(1)
    lc = SSD_CHUNK
    row = lax.broadcasted_iota(jnp.int32, (lc, lc), 0)
    col = lax.broadcasted_iota(jnp.int32, (lc, lc), 1)
    tril = row >= col
    lane = lax.broadcasted_iota(jnp.int32, (1, LANES), 1)

    @pl.when(si == 0)
    def _():
        state_ref[...] = jnp.zeros(state_ref.shape, F32)
        u_ref[0:CONV_HALO, :] = jnp.zeros((CONV_HALO, SSD_IN_COLS), F32)

    def project(k):
        r0 = CONV_HALO + k * ts
        u_ref[r0:r0 + ts, :] = _dot(h_ref[0, k * ts:(k + 1) * ts, :], win_ref[...])

    def work(k):
        r0 = CONV_HALO + k * ts
        xbc_ref[k] = _silu(_causal_conv(u_ref, r0, MIX, SSD_XBC, cw_ref, cb_ref, ts))
        dt_all = _softplus(u_ref[r0:r0 + ts, MIX + SSD_XBC:] + dtb_ref[...])
        a_all = dt_all * (-jnp.exp(alog_ref[...]))
        ys = []
        for c in range(ts // lc):
            c0 = c * lc
            xs = xbc_ref[k, c0:c0 + lc, 0:MIX]
            bm = xbc_ref[k, c0:c0 + lc, MIX:MIX + LANES]
            cm = xbc_ref[k, c0:c0 + lc, MIX + LANES:MIX + 2 * LANES]
            dt = dt_all[c0:c0 + lc]
            cs = _sel_dot_left(tri_ref[...], a_all[c0:c0 + lc])
            cs_t = cs.T
            dt_t = dt.T
            cs_end = cs[lc - 1:lc, :]
            per_head = jnp.concatenate([jnp.exp(cs), dt * jnp.exp(cs_end - cs)], axis=0)
            wide = _sel_dot2(per_head, expand_ref[...])
            ecs_x, dd_x = wide[0:lc], wide[lc:2 * lc]
            xs_b = xs.astype(BF16)
            xdd = (xs * dd_x).astype(BF16)
            y_parts = []
            for grp in range(SSD_GROUPS):
                in_grp = (lane >= grp * SSD_STATE) & (lane < (grp + 1) * SSD_STATE)
                cm_g = jnp.where(in_grp, cm, 0.0).astype(BF16)
                bm_g = jnp.where(in_grp, bm, 0.0).astype(BF16)
                cb = _dot_nt(cm_g, bm_g)
                gc = grp * SSD_GROUP_COLS
                masks = []
                for hl in range(SSD_GROUP_HEADS):
                    hd = grp * SSD_GROUP_HEADS + hl
                    seg = jnp.where(tril, cs[:, hd:hd + 1] - cs_t[hd:hd + 1, :], NEG_BIG)
                    masks.append((cb * jnp.exp(seg) * dt_t[hd:hd + 1, :]).astype(BF16))
                m_cat = jnp.concatenate(masks, axis=1)
                x_bd = jnp.tile(xs_b[:, gc:gc + SSD_GROUP_COLS], (SSD_GROUP_HEADS, 1)) * bd_ref[...]
                y_diag = _dot(m_cat, x_bd)
                st = state_ref[grp]
                y_off = _dot(cm_g, st.astype(BF16)) * ecs_x[:, gc:gc + SSD_GROUP_COLS]
                state_ref[grp] = (st * ecs_x[lc - 1:lc, gc:gc + SSD_GROUP_COLS]
                                  + _dot_tn(bm_g, xdd[:, gc:gc + SSD_GROUP_COLS]))
                y_parts.append(y_diag + y_off)
            ys.append(jnp.concatenate(y_parts, axis=1) + xs * dskip_ref[...])
        y = jnp.concatenate(ys, axis=0) * _silu(u_ref[r0:r0 + ts, 0:MIX])
        y_ref[k * ts:(k + 1) * ts, :] = _rmsnorm(y, ng_ref[...]).astype(BF16)

    _run_ahead(sub, project, work)
    u_ref[0:CONV_HALO, :] = u_ref[sub * ts:sub * ts + CONV_HALO, :]
    o_ref[0] = y_ref[...]


def _sel_dot_left(sel, a):
    hi, mid, lo = _split3(a)
    return _dot(sel, hi) + _dot(sel, mid) + _dot(sel, lo)


def _ssd_constants():
    lc = SSD_CHUNK
    tri = (jnp.arange(lc)[:, None] >= jnp.arange(lc)[None, :]).astype(BF16)
    expand = (jnp.arange(LANES)[:, None] == (jnp.arange(MIX)[None, :] // SSD_HEADDIM)).astype(BF16)
    rows = jnp.arange(SSD_GROUP_HEADS * lc)[:, None] // lc
    cols = jnp.arange(SSD_GROUP_COLS)[None, :] // SSD_HEADDIM
    bd = (rows == cols).astype(BF16)
    return tri, expand, bd


def _ssd(h, l, win, cw, cb, dtb, alog, dskip, ng):
    b, s, d = h.shape
    ts = min(SEQ_TILE, s)
    sub = _sub_tiles(s, ts, SSD_SUB)
    rows = sub * ts
    tri, expand, bd = _ssd_constants()
    return pl.pallas_call(
        functools.partial(_ssd_kernel, ts=ts, sub=sub),
        grid=(b, s // rows),
        in_specs=[pl.BlockSpec((1, rows, d), lambda i, j: (i, j, 0)), _layer(win, l), _layer(cw, l), _layer(cb, l),
                  _layer(dtb, l), _layer(alog, l), _layer(dskip, l), _layer(ng, l),
                  _full(tri.shape), _full(expand.shape), _full(bd.shape)],
        out_specs=pl.BlockSpec((1, rows, MIX), lambda i, j: (i, j, 0)),
        out_shape=jax.ShapeDtypeStruct((b, s, MIX), BF16),
        scratch_shapes=[pltpu.VMEM((CONV_HALO + rows, SSD_IN_COLS), F32),
                        pltpu.VMEM((sub, ts, SSD_XBC), F32),
                        pltpu.VMEM((rows, MIX), BF16),
                        pltpu.VMEM((SSD_GROUPS, LANES, SSD_GROUP_COLS), F32)],
        compiler_params=_params("arbitrary", "arbitrary"),
        name="ssd_mixer",
    )(h, win, cw, cb, dtb, alog, dskip, ng, tri, expand, bd)


def _lru_kernel(hn_ref, win_ref, cw_ref, cb_ref, wai_ref, bai_ref, lam_ref, o_ref,
                u_ref, a_ref, h_ref, y_ref, carry_ref, *, ts, sub):
    si = pl.program_id(1)
    rowi = lax.broadcasted_iota(jnp.int32, (SUBLANES, MIX), 0)

    @pl.when(si == 0)
    def _():
        carry_ref[...] = jnp.zeros(carry_ref.shape, F32)
        u_ref[0:CONV_HALO, :] = jnp.zeros((CONV_HALO, 2 * MIX), F32)

    def project(k):
        r0 = CONV_HALO + k * ts
        u_ref[r0:r0 + ts, :] = _dot(hn_ref[0, k * ts:(k + 1) * ts, :], win_ref[...])

    def work(k):
        r0 = CONV_HALO + k * ts
        xc = _causal_conv(u_ref, r0, MIX, MIX, cw_ref, cb_ref, ts)
        ri = _dot(xc.astype(BF16), wai_ref[...]) + bai_ref[...]
        r_t = _sigmoid(ri[:, :MIX])
        i_t = _sigmoid(ri[:, MIX:])
        log_a = (-LRU_C) * r_t * _softplus(-lam_ref[...])
        a_t = jnp.exp(log_a)
        mult = jnp.sqrt(jnp.tanh(-log_a) * (a_t * a_t + 1.0))
        a_ref[k] = a_t
        h_ref[k] = xc * i_t * mult
        carry = carry_ref[...]
        for blk in range(ts // SUBLANES):
            b0 = blk * SUBLANES
            a8 = a_ref[k, b0:b0 + SUBLANES, :]
            u8 = h_ref[k, b0:b0 + SUBLANES, :]
            for kk in (1, 2, 4):
                keep = rowi >= kk
                u8 = jnp.where(keep, a8 * pltpu.roll(u8, kk, 0) + u8, u8)
                a8 = jnp.where(keep, a8 * pltpu.roll(a8, kk, 0), a8)
            h8 = u8 + a8 * carry
            h_ref[k, b0:b0 + SUBLANES, :] = h8
            carry = h8[SUBLANES - 1:SUBLANES, :]
        carry_ref[...] = carry
        y_ref[k * ts:(k + 1) * ts, :] = (h_ref[k] * _gelu_tanh(u_ref[r0:r0 + ts, 0:MIX])).astype(BF16)

    _run_ahead(sub, project, work)
    u_ref[0:CONV_HALO, :] = u_ref[sub * ts:sub * ts + CONV_HALO, :]
    o_ref[0] = y_ref[...]


def _block_diag(w):
    depth, nb, n, _ = w.shape
    eye = jnp.eye(nb, dtype=w.dtype)
    return (eye[:, None, :, None] * w[:, :, :, None, :]).reshape(depth, nb * n, nb * n)


def _lru(h, l, win, cw, cb, wai, bai, lam):
    b, s, d = h.shape
    ts = min(SEQ_TILE, s)
    sub = _sub_tiles(s, ts, LRU_SUB)
    rows = sub * ts
    return pl.pallas_call(
        functools.partial(_lru_kernel, ts=ts, sub=sub),
        grid=(b, s // rows),
        in_specs=[pl.BlockSpec((1, rows, d), lambda i, j: (i, j, 0)), _layer(win, l), _layer(cw, l), _layer(cb, l),
                  _layer(wai, l), _layer(bai, l), _layer(lam, l)],
        out_specs=pl.BlockSpec((1, rows, MIX), lambda i, j: (i, j, 0)),
        out_shape=jax.ShapeDtypeStruct((b, s, MIX), BF16),
        scratch_shapes=[pltpu.VMEM((CONV_HALO + rows, 2 * MIX), F32), pltpu.VMEM((sub, ts, MIX), F32),
                        pltpu.VMEM((sub, ts, MIX), F32), pltpu.VMEM((rows, MIX), BF16), pltpu.VMEM((1, MIX), F32)],
        compiler_params=_params("arbitrary", "arbitrary"),
        name="rglru_mixer",
    )(h, win, cw, cb, wai, bai, lam)


def _merge_kernel(x_ref, h_ref, wg_ref, ya_ref, yb_ref, yc_ref, yd_ref, wb_ref, wo_ref, o_ref):
    h = h_ref[...]
    merged = None
    for n, y_ref in enumerate((ya_ref, yb_ref, yc_ref, yd_ref)):
        gate = _sigmoid(_dot(h, wg_ref[:, n * D_MODEL:(n + 1) * D_MODEL]))
        term = gate * _dot(y_ref[...], wb_ref[n])
        merged = term if merged is None else merged + term
    o_ref[...] = x_ref[...] + _dot(merged.astype(BF16), wo_ref[...])


def _merge(x2, h2, l, wg, ys, wb, wo):
    t, d = x2.shape
    tm = min(TOK_TILE, t)
    row = lambda w: pl.BlockSpec((tm, w), lambda i: (i, 0))
    return pl.pallas_call(
        _merge_kernel,
        grid=(t // tm,),
        in_specs=[row(d), row(d), _layer(wg, l), row(MIX), row(MIX), row(MIX), row(MIX),
                  _layer(wb, l), _layer(wo, l)],
        out_specs=row(d),
        out_shape=jax.ShapeDtypeStruct((t, d), F32),
        compiler_params=_params("arbitrary"),
        name="gated_merge",
    )(x2, h2, wg, *ys, wb, wo)


FF_CHUNK = 1024


def _mlp_ple_kernel(x_ref, g_ref, w1_ref, w2_ref, gp_ref, wpg_ref, p_ref, wple_ref, gn_ref, *out_refs, last):
    x = x_ref[...]
    h = _rmsnorm(x, g_ref[...]).astype(BF16)
    acc = x
    for c in range(D_FF // FF_CHUNK):
        hid = jnp.maximum(_dot(h, w1_ref[:, c * FF_CHUNK:(c + 1) * FF_CHUNK]), 0.0)
        acc = acc + _dot((hid * hid).astype(BF16), w2_ref[c * FF_CHUNK:(c + 1) * FF_CHUNK, :])
    gate = _sigmoid(_dot(_rmsnorm(acc, gp_ref[...]).astype(BF16), wpg_ref[...]))
    y = acc + _dot(p_ref[0].astype(BF16), wple_ref[...]) * gate
    normed = _rmsnorm(y, gn_ref[...])
    if last:
        out_refs[0][...] = normed
    else:
        out_refs[0][...] = y
        out_refs[1][...] = normed.astype(BF16)


def _mlp_ple(x2, l, g, w1, w2, gp, wpg, p3, wple, g_next, last):
    t, d = x2.shape
    tm = min(TOK_TILE, t)
    row = lambda w: pl.BlockSpec((tm, w), lambda i: (i, 0))
    out_specs = [row(d)] if last else [row(d), row(d)]
    out_shape = [jax.ShapeDtypeStruct((t, d), F32)] + ([] if last else [jax.ShapeDtypeStruct((t, d), BF16)])
    return pl.pallas_call(
        functools.partial(_mlp_ple_kernel, last=last),
        grid=(t // tm,),
        in_specs=[row(d), _layer(g, l), _layer(w1, l, True), _layer(w2, l, True), _layer(gp, l),
                  _layer(wpg, l, True), pl.BlockSpec((1, tm, PLE_DIM), lambda i: (l, i, 0)),
                  _layer(wple, l, True), _layer(g_next, 0)],
        out_specs=out_specs,
        out_shape=out_shape,
        compiler_params=_params("arbitrary"),
        name="mlp_ple",
    )(x2, g, w1, w2, gp, wpg, p3, wple, g_next)


_SPLIT = (Q_LORA, KV_LORA, QK_ROPE, MIX, MIX, SSD_XBC, SSD_HEADS, MIX, MIX, N_BRANCH * D_MODEL)


def _col_offsets():
    offs, acc = [], 0
    for sz in _SPLIT:
        offs.append(acc)
        acc += sz
    return offs


def kernel(x, p, positions, g_mix, w_in, q_norm, w_uq, kv_norm, w_ukv, w_pool, pool_scale,
           ssd_conv_w, ssd_conv_b, ssd_dt_bias, ssd_a_log, ssd_d, ssd_norm,
           lru_conv_w, lru_conv_b, lru_w_a, lru_b_a, lru_w_i, lru_b_i, lru_lambda,
           w_branch, w_out, g_mlp, w_ff1, w_ff2, g_ple, w_ple_gate, w_ple, g_final):
    b, s, d = x.shape
    depth = w_in.shape[0]
    t = b * s
    offs = _col_offsets()
    o_pool, o_z, o_xbc, o_dt, o_lg, o_lx, o_gate = offs[3], offs[4], offs[5], offs[6], offs[7], offs[8], offs[9]
    cosq, sinq = _rope_tables(positions)
    bf = lambda a: a.astype(BF16)
    vec = lambda a: a.reshape(depth, 1, -1)
    win_a, wq, wk, wv = _mla_weights(w_in, w_uq, w_ukv)
    win_b = bf(w_in[..., o_pool:o_pool + MIX])
    w_dt = jnp.pad(w_in[..., o_dt:o_dt + SSD_HEADS], ((0, 0), (0, 0), (0, LANES - SSD_HEADS)))
    win_c = bf(jnp.concatenate([w_in[..., o_z:o_z + MIX], w_in[..., o_xbc:o_xbc + SSD_XBC], w_dt], axis=-1))
    head_lanes = lambda a: vec(jnp.pad(a, ((0, 0), (0, LANES - SSD_HEADS))))
    win_d = bf(w_in[..., o_lg:o_lg + 2 * MIX])
    wai = bf(jnp.concatenate([_block_diag(lru_w_a), _block_diag(lru_w_i)], axis=-1))
    bai = vec(jnp.concatenate([lru_b_a, lru_b_i], axis=-1))
    wg, wb, wo = bf(w_in[..., o_gate:]), bf(w_branch), bf(w_out)
    w1, w2, wpg, wple = bf(w_ff1), bf(w_ff2), bf(w_ple_gate), bf(w_ple)
    gm, gmlp, gple = vec(g_mix), vec(g_mlp), vec(g_ple)
    qn, kvn, ps = vec(q_norm), vec(kv_norm), vec(pool_scale)
    scb, dtb, alog = vec(ssd_conv_b), head_lanes(ssd_dt_bias), head_lanes(ssd_a_log)
    dskip, sng = vec(jnp.repeat(ssd_d, SSD_HEADDIM, axis=-1)), vec(ssd_norm)
    lcb, lam = vec(lru_conv_b), vec(lru_lambda)
    wp = bf(w_pool)
    p3 = p.reshape(depth, t, PLE_DIM)
    x2 = x.reshape(t, d)
    h2 = _prenorm(x2, 0, gm)
    for l in range(depth):
        h = h2.reshape(b, s, d)
        q, k, v = _mla_proj(h, l, win_a, qn, wq, kvn, wk, wv, cosq, sinq)
        y_a = _attention(q, k, v)
        y_b = _pool(h, l, win_b, wp, ps)
        y_c = _ssd(h, l, win_c, ssd_conv_w, scb, dtb, alog, dskip, sng)
        y_d = _lru(h, l, win_d, lru_conv_w, lcb, wai, bai, lam)
        ys = [y.reshape(t, MIX) for y in (y_a, y_b, y_c, y_d)]
        x2 = _merge(x2, h2, l, wg, ys, wb, wo)
        last = l == depth - 1
        g_next = g_final.reshape(1, 1, d) if last else gm[l + 1:l + 2]
        outs = _mlp_ple(x2, l, gmlp, w1, w2, gple, wpg, p3, wple, g_next, last)
        if last:
            x2 = outs[0]
        else:
            x2, h2 = outs
    return x2.reshape(b, s, d)
```

```python
import functools

import jax
import jax.numpy as jnp
import numpy as np
from jax import lax
from jax.experimental import pallas as pl
from jax.experimental.pallas import tpu as pltpu

F32 = jnp.float32
BF16 = jnp.bfloat16

D_MODEL = 1024
MIX = 512
N_BRANCH = 4
HEADS = 8
QK_NOPE = 64
QK_ROPE = 32
V_HEAD = 64
Q_LORA = 384
KV_LORA = 256
ROPE_THETA = 10000.0
POOL_WINDOWS = (2, 4, 8, 16)
POOL_GROUP = 128
POOL_HALO = 16
SSD_HEADS = 8
SSD_HEADDIM = 64
SSD_GROUPS = 2
SSD_STATE = 64
SSD_CHUNK = 128
CONV_WIDTH = 4
CONV_HALO = 8
SSD_XBC = 768
LRU_BLOCKS = 8
LRU_BLOCK = 64
LRU_C = 8.0
D_FF = 4096
PLE_DIM = 256
EPS = 1e-6
LANES = 128
SUBLANES = 8
HEAD_PAD = 128
NEG_BIG = -1e30
LOG2_E = 1.4426950408889634
VMEM_LIMIT = 56 * 1024 * 1024

SEQ_TILE = 512
POOL_SUB = 4
SSD_SUB = 2
LRU_SUB = 4
NORM_TILE = 2048
TOK_TILE = 1024
ATT_TILE = 512
ATT_HEADS = 4
ATT_SKEW = 1


def _dot(a, b):
    return jnp.dot(a, b, preferred_element_type=F32)


def _dot_nt(a, b):
    return lax.dot_general(a, b, (((1,), (1,)), ((), ())), preferred_element_type=F32)


def _dot_tn(a, b):
    return lax.dot_general(a, b, (((0,), (0,)), ((), ())), preferred_element_type=F32)


def _split3(a):
    hi = a.astype(BF16)
    r1 = a - hi.astype(F32)
    mid = r1.astype(BF16)
    lo = (r1 - mid.astype(F32)).astype(BF16)
    return hi, mid, lo


def _sel_dot2(a, sel):
    hi = a.astype(BF16)
    mid = (a - hi.astype(F32)).astype(BF16)
    return _dot(hi, sel) + _dot(mid, sel)


def _rmsnorm(x, g):
    return x * lax.rsqrt(jnp.mean(x * x, axis=-1, keepdims=True) + EPS) * g


def _sigmoid(x):
    return jax.nn.sigmoid(x)


def _silu(x):
    return x * jax.nn.sigmoid(x)


def _softplus(x):
    return jnp.maximum(x, 0.0) + jnp.log1p(jnp.exp(-jnp.abs(x)))


def _gelu_tanh(x):
    return 0.5 * x * (1.0 + jnp.tanh(0.7978845608028654 * (x + 0.044715 * (x * x * x))))


def _params(*sem):
    return pltpu.CompilerParams(dimension_semantics=sem, vmem_limit_bytes=VMEM_LIMIT)


def _full(shape):
    n = len(shape)
    return pl.BlockSpec(shape, lambda *_: (0,) * n)


def _layer(stack, l, resident=False):
    n = stack.ndim - 1
    mode = dict(pipeline_mode=pl.Buffered(1)) if resident else {}
    return pl.BlockSpec((None,) + tuple(stack.shape[1:]), lambda *_: (l,) + (0,) * n, **mode)


def _prenorm_kernel(x_ref, g_ref, h_ref):
    h_ref[...] = _rmsnorm(x_ref[...], g_ref[...]).astype(BF16)


def _prenorm(x2, l, g):
    t, d = x2.shape
    tm = min(NORM_TILE, t)
    row = pl.BlockSpec((tm, d), lambda i: (i, 0))
    return pl.pallas_call(
        _prenorm_kernel,
        grid=(t // tm,),
        in_specs=[row, _layer(g, l)],
        out_specs=row,
        out_shape=jax.ShapeDtypeStruct((t, d), BF16),
        compiler_params=_params("arbitrary"),
        name="prenorm",
    )(x2, g)


ROPE_HALF = QK_ROPE // 2
ROPE_PACK = LANES // ROPE_HALF


def _rope_kernel(ang_ref, place_ref, cos_ref, sin_ref, *, blk):
    a = ang_ref[...]
    cos_parts = _split3(jnp.cos(a))
    sin_parts = _split3(jnp.sin(a))
    lane = lax.broadcasted_iota(jnp.int32, (1, LANES), 1)
    ones = (lane < QK_NOPE).astype(F32)
    for i in range(ROPE_PACK):
        ci = sum(_dot(part, place_ref[0, i]) for part in cos_parts) + ones
        si = sum(_dot(part, place_ref[1, i]) for part in sin_parts)
        cos_ref[pl.ds(i, blk, stride=ROPE_PACK), :] = ci
        sin_ref[pl.ds(i, blk, stride=ROPE_PACK), :] = si


def _rope_placement():
    place = np.zeros((2, ROPE_PACK, LANES, LANES), np.float32)
    for i in range(ROPE_PACK):
        for j in range(ROPE_HALF):
            src = i * ROPE_HALF + j
            place[0, i, src, QK_NOPE + j] = 1.0
            place[0, i, src, QK_NOPE + ROPE_HALF + j] = 1.0
            place[1, i, src, QK_NOPE + j] = -1.0
            place[1, i, src, QK_NOPE + ROPE_HALF + j] = 1.0
    return jnp.asarray(place, BF16)


def _rope_tables(positions):
    b, s = positions.shape
    t = b * s
    inv = 1.0 / (ROPE_THETA ** (jnp.arange(0, QK_ROPE, 2, dtype=F32) / QK_ROPE))
    rows = t // ROPE_PACK
    pos = positions.astype(F32).reshape(rows, ROPE_PACK)
    ang = jnp.repeat(pos, ROPE_HALF, axis=1) * jnp.tile(inv, ROPE_PACK)[None, :]
    blk = min(rows, 512)
    place = _rope_placement()
    out_spec = pl.BlockSpec((blk * ROPE_PACK, LANES), lambda i: (i, 0))
    cosq, sinq = pl.pallas_call(
        functools.partial(_rope_kernel, blk=blk),
        grid=(rows // blk,),
        in_specs=[pl.BlockSpec((blk, LANES), lambda i: (i, 0)), _full(place.shape)],
        out_specs=[out_spec, out_spec],
        out_shape=[jax.ShapeDtypeStruct((t, LANES), F32)] * 2,
        compiler_params=_params("arbitrary"),
        name="rope_tables",
    )(ang, place)
    return cosq.reshape(b, s, LANES), sinq.reshape(b, s, LANES)


MLA_IN_COLS = Q_LORA + KV_LORA + 2 * HEAD_PAD


def _mla_proj_kernel(h_ref, win_ref, qn_ref, wq_ref, kvn_ref, wk_ref, wv_ref, cos_ref, sin_ref,
                     q_out, k_out, v_out):
    scale = (QK_NOPE + QK_ROPE) ** -0.5 * LOG2_E
    u = _dot(h_ref[0], win_ref[...])
    c_q = u[:, :Q_LORA]
    c_kv = u[:, Q_LORA:Q_LORA + KV_LORA]
    kr = u[:, Q_LORA + KV_LORA:Q_LORA + KV_LORA + HEAD_PAD]
    kr_sw = u[:, Q_LORA + KV_LORA + HEAD_PAD:]
    cosq = cos_ref[0]
    sinq = sin_ref[0]
    cqn = _rmsnorm(c_q, qn_ref[...]).astype(BF16)
    qq = _dot(cqn, wq_ref[...])
    ckvn = _rmsnorm(c_kv, kvn_ref[...]).astype(BF16)
    kn = _dot(ckvn, wk_ref[...])
    v = _dot(ckvn, wv_ref[...])
    lane = lax.broadcasted_iota(jnp.int32, (1, HEAD_PAD), 1)
    ones_lane = (lane == V_HEAD).astype(F32)
    k_rope = kr * cosq + kr_sw * sinq
    wide = HEADS * HEAD_PAD
    for hd in range(HEADS):
        lo = hd * HEAD_PAD
        q_h = (qq[:, lo:lo + HEAD_PAD] * cosq + qq[:, wide + lo:wide + lo + HEAD_PAD] * sinq) * scale
        q_out[0, hd] = q_h.astype(BF16)
        k_out[0, hd] = (kn[:, lo:lo + HEAD_PAD] + k_rope).astype(BF16)
        v_out[0, hd] = (v[:, lo:lo + HEAD_PAD] + ones_lane).astype(BF16)


def _mla_weights(w_in, w_uq, w_ukv):
    depth = w_in.shape[0]
    half = QK_ROPE // 2
    pad_l, pad_r = QK_NOPE, HEAD_PAD - QK_NOPE - QK_ROPE
    o = Q_LORA + KV_LORA
    w_kr = w_in[..., o:o + QK_ROPE]
    w_kr_sw = jnp.concatenate([w_kr[..., half:], w_kr[..., :half]], axis=-1)
    place = lambda w: jnp.pad(w, ((0, 0), (0, 0), (pad_l, pad_r)))
    win = jnp.concatenate([w_in[..., :o], place(w_kr), place(w_kr_sw)], axis=-1).astype(BF16)
    wq4 = w_uq.reshape(depth, Q_LORA, HEADS, QK_NOPE + QK_ROPE)
    wq_a = jnp.pad(wq4, ((0, 0), (0, 0), (0, 0), (0, pad_r))).reshape(depth, Q_LORA, HEADS * HEAD_PAD)
    rope = wq4[..., QK_NOPE:]
    rope_sw = jnp.concatenate([rope[..., half:], rope[..., :half]], axis=-1)
    wq_b = jnp.pad(rope_sw, ((0, 0), (0, 0), (0, 0), (pad_l, pad_r))).reshape(depth, Q_LORA, HEADS * HEAD_PAD)
    wq = jnp.concatenate([wq_a, wq_b], axis=-1).astype(BF16)
    wkv4 = w_ukv.reshape(depth, KV_LORA, HEADS, QK_NOPE + V_HEAD)
    pad_head = lambda w, n: jnp.pad(w, ((0, 0), (0, 0), (0, 0), (0, HEAD_PAD - n))).reshape(
        depth, KV_LORA, HEADS * HEAD_PAD).astype(BF16)
    return win, wq, pad_head(wkv4[..., :QK_NOPE], QK_NOPE), pad_head(wkv4[..., QK_NOPE:], V_HEAD)


def _mla_proj(h, l, win, qn, wq, kvn, wk, wv, cosq, sinq):
    b, s, d = h.shape
    ts = min(TOK_TILE, s)
    tile = lambda w: pl.BlockSpec((1, ts, w), lambda i, j: (i, j, 0))
    head_out = lambda n: pl.BlockSpec((1, n, ts, HEAD_PAD), lambda i, j: (i, 0, j, 0))
    return pl.pallas_call(
        _mla_proj_kernel,
        grid=(b, s // ts),
        in_specs=[tile(d), _layer(win, l), _layer(qn, l), _layer(wq, l), _layer(kvn, l), _layer(wk, l),
                  _layer(wv, l), tile(HEAD_PAD), tile(HEAD_PAD)],
        out_specs=[head_out(HEADS)] * 3,
        out_shape=[jax.ShapeDtypeStruct((b, HEADS, s, HEAD_PAD), BF16)] * 3,
        compiler_params=_params("arbitrary", "arbitrary"),
        name="mla_proj",
    )(h, win, qn, wq, kvn, wk, wv, cosq, sinq)


def _attn_kernel(q_ref, k_ref, v_ref, o_ref, sa_ref, sb_ref, m_ref, acc_ref, *, tq, nq):
    row = lax.broadcasted_iota(jnp.int32, (tq, tq), 0)
    col = lax.broadcasted_iota(jnp.int32, (tq, tq), 1)
    causal = row >= col
    lane = lax.broadcasted_iota(jnp.int32, (tq, LANES), 1)
    reps = tq // LANES

    def produce(dst_ref, hh, qi, kb):
        q0 = pl.multiple_of(qi * tq, tq)
        k0 = pl.multiple_of(kb * tq, tq)
        dst_ref[hh] = _dot_nt(q_ref[0, hh, pl.ds(q0, tq), :], k_ref[0, hh, pl.ds(k0, tq), :])

    def consume(src_ref, hh, kb, masked):
        k0 = pl.multiple_of(kb * tq, tq)
        sc = src_ref[hh]
        if masked:
            sc = jnp.where(causal, sc, NEG_BIG)
        m = m_ref[hh]
        m_new = jnp.maximum(m, jnp.max(sc, axis=-1, keepdims=True))
        p = jnp.exp2(sc - jnp.tile(m_new, (1, reps))).astype(BF16)
        acc_ref[hh] = jnp.exp2(m - m_new) * acc_ref[hh] + _dot(p, v_ref[0, hh, pl.ds(k0, tq), :])
        m_ref[hh] = m_new

    def step(src_ref, dst_ref, kb, masked, nqi, nkb):
        for hh in range(min(ATT_SKEW, ATT_HEADS)):
            produce(dst_ref, hh, nqi, nkb)
        for hh in range(ATT_HEADS):
            consume(src_ref, hh, kb, masked)
            if hh + ATT_SKEW < ATT_HEADS:
                produce(dst_ref, hh + ATT_SKEW, nqi, nkb)

    def reset():
        m_ref[...] = jnp.full(m_ref.shape, NEG_BIG, F32)
        acc_ref[...] = jnp.zeros(acc_ref.shape, F32)

    def finalize(qi):
        q0 = pl.multiple_of(qi * tq, tq)
        for pair in range(ATT_HEADS // 2):
            acc0 = acc_ref[2 * pair]
            acc1 = acc_ref[2 * pair + 1]
            o0 = acc0 / acc0[:, V_HEAD:V_HEAD + 1]
            o1 = acc1 / acc1[:, V_HEAD:V_HEAD + 1]
            o_ref[0, pl.ds(q0, tq), pair * LANES:(pair + 1) * LANES] = jnp.where(
                lane < V_HEAD, o0, pltpu.roll(o1, V_HEAD, 1)).astype(BF16)

    n_steps = nq * (nq + 1) // 2
    bufs = (sa_ref, sb_ref)
    reset()
    for hh in range(ATT_HEADS):
        produce(sa_ref, hh, 0, 0)

    def body(t, carry):
        qi, kb = carry
        last = kb == qi
        nqi = jnp.where(last, qi + 1, qi)
        nkb = jnp.where(last, 0, kb + 1)
        for par in range(2):
            src, dst = bufs[par], bufs[1 - par]

            @pl.when((t % 2 == par) & jnp.logical_not(last))
            def _():
                step(src, dst, kb, False, nqi, nkb)

            @pl.when((t % 2 == par) & last)
            def _():
                step(src, dst, kb, True, nqi, nkb)
                finalize(qi)
                reset()

        return nqi, nkb

    lax.fori_loop(0, n_steps - 1, body, (jnp.int32(0), jnp.int32(0)))
    for hh in range(ATT_HEADS):
        consume(bufs[(n_steps - 1) % 2], hh, nq - 1, True)
    finalize(nq - 1)


def _attention(q, k, v):
    b, _, s, _ = q.shape
    tq = min(ATT_TILE, s)
    spec = pl.BlockSpec((1, ATT_HEADS, s, HEAD_PAD), lambda i, hg: (i, hg, 0, 0))
    return pl.pallas_call(
        functools.partial(_attn_kernel, tq=tq, nq=s // tq),
        grid=(b, HEADS // ATT_HEADS),
        in_specs=[spec, spec, spec],
        out_specs=pl.BlockSpec((1, s, ATT_HEADS * V_HEAD), lambda i, hg: (i, 0, hg)),
        out_shape=jax.ShapeDtypeStruct((b, s, MIX), BF16),
        scratch_shapes=[pltpu.VMEM((ATT_HEADS, tq, tq), F32), pltpu.VMEM((ATT_HEADS, tq, tq), F32),
                        pltpu.VMEM((ATT_HEADS, tq, LANES), F32), pltpu.VMEM((ATT_HEADS, tq, HEAD_PAD), F32)],
        compiler_params=_params("arbitrary", "arbitrary"),
        name="mla_attention",
    )(q, k, v)


def _sub_tiles(seq_len, ts, want):
    n = seq_len // ts
    return max(k for k in range(1, want + 1) if n % k == 0)


def _run_ahead(sub, project, work):
    project(0)
    for k in range(sub):
        if k + 1 < sub:
            project(k + 1)
        work(k)


def _pool_kernel(h_ref, win_ref, wp_ref, ps_ref, o_ref, ext_ref, y_ref, *, ts, sub):
    si = pl.program_id(1)

    @pl.when(si == 0)
    def _():
        ext_ref[0:POOL_HALO, :] = jnp.zeros((POOL_HALO, MIX), F32)

    def project(k):
        r0 = POOL_HALO + k * ts
        ext_ref[r0:r0 + ts, :] = _dot(h_ref[0, k * ts:(k + 1) * ts, :], win_ref[...])

    def work(k):
        r0 = POOL_HALO + k * ts
        t = (si * sub + k) * ts + lax.broadcasted_iota(jnp.int32, (ts, 1), 0)
        for g, w in enumerate(POOL_WINDOWS):
            lo = g * POOL_GROUP
            win = ext_ref[r0 - POOL_HALO:r0 + ts, lo:lo + POOL_GROUP]
            u_g = win[POOL_HALO:]
            shift = 1
            while shift < w:
                win = win + pltpu.roll(win, shift, 0)
                shift *= 2
            win = win[POOL_HALO:]
            count = jnp.minimum(t + 1, w).astype(F32)
            d = win / count - u_g
            y = _dot(d.astype(BF16), wp_ref[g])
            y_ref[k * ts:(k + 1) * ts, lo:lo + POOL_GROUP] = (y * ps_ref[:, lo:lo + POOL_GROUP]).astype(BF16)

    _run_ahead(sub, project, work)
    ext_ref[0:POOL_HALO, :] = ext_ref[sub * ts:sub * ts + POOL_HALO, :]
    o_ref[0] = y_ref[...]


def _pool(h, l, win, wp, ps):
    b, s, d = h.shape
    ts = min(SEQ_TILE, s)
    sub = _sub_tiles(s, ts, POOL_SUB)
    rows = sub * ts
    return pl.pallas_call(
        functools.partial(_pool_kernel, ts=ts, sub=sub),
        grid=(b, s // rows),
        in_specs=[pl.BlockSpec((1, rows, d), lambda i, j: (i, j, 0)), _layer(win, l), _layer(wp, l), _layer(ps, l)],
        out_specs=pl.BlockSpec((1, rows, MIX), lambda i, j: (i, j, 0)),
        out_shape=jax.ShapeDtypeStruct((b, s, MIX), BF16),
        scratch_shapes=[pltpu.VMEM((POOL_HALO + rows, MIX), F32), pltpu.VMEM((rows, MIX), BF16)],
        compiler_params=_params("arbitrary", "arbitrary"),
        name="pool_mixer",
    )(h, win, wp, ps)


def _causal_conv(buf_ref, r0, c0, width, cw_ref, cb_ref, ts):
    def rows(back):
        return buf_ref[r0 - back:r0 - back + ts, c0:c0 + width]

    acc = rows(0) * cw_ref[CONV_WIDTH - 1:CONV_WIDTH, :] + cb_ref[...]
    for jj in range(CONV_WIDTH - 1):
        acc = acc + rows(CONV_WIDTH - 1 - jj) * cw_ref[jj:jj + 1, :]
    return acc


SSD_IN_COLS = MIX + SSD_XBC + LANES
SSD_GROUP_COLS = MIX // SSD_GROUPS
SSD_GROUP_HEADS = SSD_HEADS // SSD_GROUPS


def _ssd_kernel(h_ref, win_ref, cw_ref, cb_ref, dtb_ref, alog_ref, dskip_ref, ng_ref,
                tri_ref, expand_ref, bd_ref, o_ref, u_ref, xbc_ref, y_ref, state_ref, *, ts, sub):
    si = pl.program_id(1)
    lc = SSD_CHUNK
    row = lax.broadcasted_iota(jnp.int32, (lc, lc), 0)
    col = lax.broadcasted_iota(jnp.int32, (lc, lc), 1)
    tril = row >= col
    lane = lax.broadcasted_iota(jnp.int32, (1, LANES), 1)

    @pl.when(si == 0)
    def _():
        state_ref[...] = jnp.zeros(state_ref.shape, F32)
        u_ref[0:CONV_HALO, :] = jnp.zeros((CONV_HALO, SSD_IN_COLS), F32)

    def project(k):
        r0 = CONV_HALO + k * ts
        u_ref[r0:r0 + ts, :] = _dot(h_ref[0, k * ts:(k + 1) * ts, :], win_ref[...])

    def work(k):
        r0 = CONV_HALO + k * ts
        xbc_ref[k] = _silu(_causal_conv(u_ref, r0, MIX, SSD_XBC, cw_ref, cb_ref, ts))
        dt_all = _softplus(u_ref[r0:r0 + ts, MIX + SSD_XBC:] + dtb_ref[...])
        a_all = dt_all * (-jnp.exp(alog_ref[...]))
        ys = []
        for c in range(ts // lc):
            c0 = c * lc
            xs = xbc_ref[k, c0:c0 + lc, 0:MIX]
            bm = xbc_ref[k, c0:c0 + lc, MIX:MIX + LANES]
            cm = xbc_ref[k, c0:c0 + lc, MIX + LANES:MIX + 2 * LANES]
            dt = dt_all[c0:c0 + lc]
            cs = _sel_dot_left(tri_ref[...], a_all[c0:c0 + lc])
            cs_t = cs.T
            dt_t = dt.T
            cs_end = cs[lc - 1:lc, :]
            per_head = jnp.concatenate([jnp.exp(cs), dt * jnp.exp(cs_end - cs)], axis=0)
            wide = _sel_dot2(per_head, expand_ref[...])
            ecs_x, dd_x = wide[0:lc], wide[lc:2 * lc]
            xs_b = xs.astype(BF16)
            xdd = (xs * dd_x).astype(BF16)
            y_parts = []
            for grp in range(SSD_GROUPS):
                in_grp = (lane >= grp * SSD_STATE) & (lane < (grp + 1) * SSD_STATE)
                cm_g = jnp.where(in_grp, cm, 0.0).astype(BF16)
                bm_g = jnp.where(in_grp, bm, 0.0).astype(BF16)
                cb = _dot_nt(cm_g, bm_g)
                gc = grp * SSD_GROUP_COLS
                masks = []
                for hl in range(SSD_GROUP_HEADS):
                    hd = grp * SSD_GROUP_HEADS + hl
                    seg = jnp.where(tril, cs[:, hd:hd + 1] - cs_t[hd:hd + 1, :], NEG_BIG)
                    masks.append((cb * jnp.exp(seg) * dt_t[hd:hd + 1, :]).astype(BF16))
                m_cat = jnp.concatenate(masks, axis=1)
                x_bd = jnp.tile(xs_b[:, gc:gc + SSD_GROUP_COLS], (SSD_GROUP_HEADS, 1)) * bd_ref[...]
                y_diag = _dot(m_cat, x_bd)
                st = state_ref[grp]
                y_off = _dot(cm_g, st.astype(BF16)) * ecs_x[:, gc:gc + SSD_GROUP_COLS]
                state_ref[grp] = (st * ecs_x[lc - 1:lc, gc:gc + SSD_GROUP_COLS]
                                  + _dot_tn(bm_g, xdd[:, gc:gc + SSD_GROUP_COLS]))
                y_parts.append(y_diag + y_off)
            ys.append(jnp.concatenate(y_parts, axis=1) + xs * dskip_ref[...])
        y = jnp.concatenate(ys, axis=0) * _silu(u_ref[r0:r0 + ts, 0:MIX])
        y_ref[k * ts:(k + 1) * ts, :] = _rmsnorm(y, ng_ref[...]).astype(BF16)

    _run_ahead(sub, project, work)
    u_ref[0:CONV_HALO, :] = u_ref[sub * ts:sub * ts + CONV_HALO, :]
    o_ref[0] = y_ref[...]


def _sel_dot_left(sel, a):
    hi, mid, lo = _split3(a)
    return _dot(sel, hi) + _dot(sel, mid) + _dot(sel, lo)


def _ssd_constants():
    lc = SSD_CHUNK
    tri = (jnp.arange(lc)[:, None] >= jnp.arange(lc)[None, :]).astype(BF16)
    expand = (jnp.arange(LANES)[:, None] == (jnp.arange(MIX)[None, :] // SSD_HEADDIM)).astype(BF16)
    rows = jnp.arange(SSD_GROUP_HEADS * lc)[:, None] // lc
    cols = jnp.arange(SSD_GROUP_COLS)[None, :] // SSD_HEADDIM
    bd = (rows == cols).astype(BF16)
    return tri, expand, bd


def _ssd(h, l, win, cw, cb, dtb, alog, dskip, ng):
    b, s, d = h.shape
    ts = min(SEQ_TILE, s)
    sub = _sub_tiles(s, ts, SSD_SUB)
    rows = sub * ts
    tri, expand, bd = _ssd_constants()
    return pl.pallas_call(
        functools.partial(_ssd_kernel, ts=ts, sub=sub),
        grid=(b, s // rows),
        in_specs=[pl.BlockSpec((1, rows, d), lambda i, j: (i, j, 0)), _layer(win, l), _layer(cw, l), _layer(cb, l),
                  _layer(dtb, l), _layer(alog, l), _layer(dskip, l), _layer(ng, l),
                  _full(tri.shape), _full(expand.shape), _full(bd.shape)],
        out_specs=pl.BlockSpec((1, rows, MIX), lambda i, j: (i, j, 0)),
        out_shape=jax.ShapeDtypeStruct((b, s, MIX), BF16),
        scratch_shapes=[pltpu.VMEM((CONV_HALO + rows, SSD_IN_COLS), F32),
                        pltpu.VMEM((sub, ts, SSD_XBC), F32),
                        pltpu.VMEM((rows, MIX), BF16),
                        pltpu.VMEM((SSD_GROUPS, LANES, SSD_GROUP_COLS), F32)],
        compiler_params=_params("arbitrary", "arbitrary"),
        name="ssd_mixer",
    )(h, win, cw, cb, dtb, alog, dskip, ng, tri, expand, bd)


def _lru_kernel(hn_ref, win_ref, cw_ref, cb_ref, wai_ref, bai_ref, lam_ref, o_ref,
                u_ref, a_ref, h_ref, y_ref, carry_ref, *, ts, sub):
    si = pl.program_id(1)
    rowi = lax.broadcasted_iota(jnp.int32, (SUBLANES, MIX), 0)

    @pl.when(si == 0)
    def _():
        carry_ref[...] = jnp.zeros(carry_ref.shape, F32)
        u_ref[0:CONV_HALO, :] = jnp.zeros((CONV_HALO, 2 * MIX), F32)

    def project(k):
        r0 = CONV_HALO + k * ts
        u_ref[r0:r0 + ts, :] = _dot(hn_ref[0, k * ts:(k + 1) * ts, :], win_ref[...])

    def work(k):
        r0 = CONV_HALO + k * ts
        xc = _causal_conv(u_ref, r0, MIX, MIX, cw_ref, cb_ref, ts)
        ri = _dot(xc.astype(BF16), wai_ref[...]) + bai_ref[...]
        r_t = _sigmoid(ri[:, :MIX])
        i_t = _sigmoid(ri[:, MIX:])
        log_a = (-LRU_C) * r_t * _softplus(-lam_ref[...])
        a_t = jnp.exp(log_a)
        mult = jnp.sqrt(jnp.tanh(-log_a) * (a_t * a_t + 1.0))
        a_ref[k] = a_t
        h_ref[k] = xc * i_t * mult
        carry = carry_ref[...]
        for blk in range(ts // SUBLANES):
            b0 = blk * SUBLANES
            a8 = a_ref[k, b0:b0 + SUBLANES, :]
            u8 = h_ref[k, b0:b0 + SUBLANES, :]
            for kk in (1, 2, 4):
                keep = rowi >= kk
                u8 = jnp.where(keep, a8 * pltpu.roll(u8, kk, 0) + u8, u8)
                a8 = jnp.where(keep, a8 * pltpu.roll(a8, kk, 0), a8)
            h8 = u8 + a8 * carry
            h_ref[k, b0:b0 + SUBLANES, :] = h8
            carry = h8[SUBLANES - 1:SUBLANES, :]
        carry_ref[...] = carry
        y_ref[k * ts:(k + 1) * ts, :] = (h_ref[k] * _gelu_tanh(u_ref[r0:r0 + ts, 0:MIX])).astype(BF16)

    _run_ahead(sub, project, work)
    u_ref[0:CONV_HALO, :] = u_ref[sub * ts:sub * ts + CONV_HALO, :]
    o_ref[0] = y_ref[...]


def _block_diag(w):
    depth, nb, n, _ = w.shape
    eye = jnp.eye(nb, dtype=w.dtype)
    return (eye[:, None, :, None] * w[:, :, :, None, :]).reshape(depth, nb * n, nb * n)


def _lru(h, l, win, cw, cb, wai, bai, lam):
    b, s, d = h.shape
    ts = min(SEQ_TILE, s)
    sub = _sub_tiles(s, ts, LRU_SUB)
    rows = sub * ts
    return pl.pallas_call(
        functools.partial(_lru_kernel, ts=ts, sub=sub),
        grid=(b, s // rows),
        in_specs=[pl.BlockSpec((1, rows, d), lambda i, j: (i, j, 0)), _layer(win, l), _layer(cw, l), _layer(cb, l),
                  _layer(wai, l), _layer(bai, l), _layer(lam, l)],
        out_specs=pl.BlockSpec((1, rows, MIX), lambda i, j: (i, j, 0)),
        out_shape=jax.ShapeDtypeStruct((b, s, MIX), BF16),
        scratch_shapes=[pltpu.VMEM((CONV_HALO + rows, 2 * MIX), F32), pltpu.VMEM((sub, ts, MIX), F32),
                        pltpu.VMEM((sub, ts, MIX), F32), pltpu.VMEM((rows, MIX), BF16), pltpu.VMEM((1, MIX), F32)],
        compiler_params=_params("arbitrary", "arbitrary"),
        name="rglru_mixer",
    )(h, win, cw, cb, wai, bai, lam)


def _merge_kernel(x_ref, h_ref, wg_ref, ya_ref, yb_ref, yc_ref, yd_ref, wb_ref, wo_ref, o_ref):
    h = h_ref[...]
    merged = None
    for n, y_ref in enumerate((ya_ref, yb_ref, yc_ref, yd_ref)):
        gate = _sigmoid(_dot(h, wg_ref[:, n * D_MODEL:(n + 1) * D_MODEL]))
        term = gate * _dot(y_ref[...], wb_ref[n])
        merged = term if merged is None else merged + term
    o_ref[...] = x_ref[...] + _dot(merged.astype(BF16), wo_ref[...])


def _merge(x2, h2, l, wg, ys, wb, wo):
    t, d = x2.shape
    tm = min(TOK_TILE, t)
    row = lambda w: pl.BlockSpec((tm, w), lambda i: (i, 0))
    return pl.pallas_call(
        _merge_kernel,
        grid=(t // tm,),
        in_specs=[row(d), row(d), _layer(wg, l), row(MIX), row(MIX), row(MIX), row(MIX),
                  _layer(wb, l), _layer(wo, l)],
        out_specs=row(d),
        out_shape=jax.ShapeDtypeStruct((t, d), F32),
        compiler_params=_params("arbitrary"),
        name="gated_merge",
    )(x2, h2, wg, *ys, wb, wo)


FF_CHUNK = 1024


def _mlp_ple_kernel(x_ref, g_ref, w1_ref, w2_ref, gp_ref, wpg_ref, p_ref, wple_ref, gn_ref, *out_refs, last):
    x = x_ref[...]
    h = _rmsnorm(x, g_ref[...]).astype(BF16)
    acc = x
    for c in range(D_FF // FF_CHUNK):
        hid = jnp.maximum(_dot(h, w1_ref[:, c * FF_CHUNK:(c + 1) * FF_CHUNK]), 0.0)
        acc = acc + _dot((hid * hid).astype(BF16), w2_ref[c * FF_CHUNK:(c + 1) * FF_CHUNK, :])
    gate = _sigmoid(_dot(_rmsnorm(acc, gp_ref[...]).astype(BF16), wpg_ref[...]))
    y = acc + _dot(p_ref[0].astype(BF16), wple_ref[...]) * gate
    normed = _rmsnorm(y, gn_ref[...])
    if last:
        out_refs[0][...] = normed
    else:
        out_refs[0][...] = y
        out_refs[1][...] = normed.astype(BF16)


def _mlp_ple(x2, l, g, w1, w2, gp, wpg, p3, wple, g_next, last):
    t, d = x2.shape
    tm = min(TOK_TILE, t)
    row = lambda w: pl.BlockSpec((tm, w), lambda i: (i, 0))
    out_specs = [row(d)] if last else [row(d), row(d)]
    out_shape = [jax.ShapeDtypeStruct((t, d), F32)] + ([] if last else [jax.ShapeDtypeStruct((t, d), BF16)])
    return pl.pallas_call(
        functools.partial(_mlp_ple_kernel, last=last),
        grid=(t // tm,),
        in_specs=[row(d), _layer(g, l), _layer(w1, l, True), _layer(w2, l, True), _layer(gp, l),
                  _layer(wpg, l, True), pl.BlockSpec((1, tm, PLE_DIM), lambda i: (l, i, 0)),
                  _layer(wple, l, True), _layer(g_next, 0)],
        out_specs=out_specs,
        out_shape=out_shape,
        compiler_params=_params("arbitrary"),
        name="mlp_ple",
    )(x2, g, w1, w2, gp, wpg, p3, wple, g_next)


_SPLIT = (Q_LORA, KV_LORA, QK_ROPE, MIX, MIX, SSD_XBC, SSD_HEADS, MIX, MIX, N_BRANCH * D_MODEL)


def _col_offsets():
    offs, acc = [], 0
    for sz in _SPLIT:
        offs.append(acc)
        acc += sz
    return offs


def kernel(x, p, positions, g_mix, w_in, q_norm, w_uq, kv_norm, w_ukv, w_pool, pool_scale,
           ssd_conv_w, ssd_conv_b, ssd_dt_bias, ssd_a_log, ssd_d, ssd_norm,
           lru_conv_w, lru_conv_b, lru_w_a, lru_b_a, lru_w_i, lru_b_i, lru_lambda,
           w_branch, w_out, g_mlp, w_ff1, w_ff2, g_ple, w_ple_gate, w_ple, g_final):
    b, s, d = x.shape
    depth = w_in.shape[0]
    t = b * s
    offs = _col_offsets()
    o_pool, o_z, o_xbc, o_dt, o_lg, o_lx, o_gate = offs[3], offs[4], offs[5], offs[6], offs[7], offs[8], offs[9]
    cosq, sinq = _rope_tables(positions)
    bf = lambda a: a.astype(BF16)
    vec = lambda a: a.reshape(depth, 1, -1)
    win_a, wq, wk, wv = _mla_weights(w_in, w_uq, w_ukv)
    win_b = bf(w_in[..., o_pool:o_pool + MIX])
    w_dt = jnp.pad(w_in[..., o_dt:o_dt + SSD_HEADS], ((0, 0), (0, 0), (0, LANES - SSD_HEADS)))
    win_c = bf(jnp.concatenate([w_in[..., o_z:o_z + MIX], w_in[..., o_xbc:o_xbc + SSD_XBC], w_dt], axis=-1))
    head_lanes = lambda a: vec(jnp.pad(a, ((0, 0), (0, LANES - SSD_HEADS))))
    win_d = bf(w_in[..., o_lg:o_lg + 2 * MIX])
    wai = bf(jnp.concatenate([_block_diag(lru_w_a), _block_diag(lru_w_i)], axis=-1))
    bai = vec(jnp.concatenate([lru_b_a, lru_b_i], axis=-1))
    wg, wb, wo = bf(w_in[..., o_gate:]), bf(w_branch), bf(w_out)
    w1, w2, wpg, wple = bf(w_ff1), bf(w_ff2), bf(w_ple_gate), bf(w_ple)
    gm, gmlp, gple = vec(g_mix), vec(g_mlp), vec(g_ple)
    qn, kvn, ps = vec(q_norm), vec(kv_norm), vec(pool_scale)
    scb, dtb, alog = vec(ssd_conv_b), head_lanes(ssd_dt_bias), head_lanes(ssd_a_log)
    dskip, sng = vec(jnp.repeat(ssd_d, SSD_HEADDIM, axis=-1)), vec(ssd_norm)
    lcb, lam = vec(lru_conv_b), vec(lru_lambda)
    wp = bf(w_pool)
    p3 = p.reshape(depth, t, PLE_DIM)
    x2 = x.reshape(t, d)
    h2 = _prenorm(x2, 0, gm)
    for l in range(depth):
        h = h2.reshape(b, s, d)
        q, k, v = _mla_proj(h, l, win_a, qn, wq, kvn, wk, wv, cosq, sinq)
        y_a = _attention(q, k, v)
        y_b = _pool(h, l, win_b, wp, ps)
        y_c = _ssd(h, l, win_c, ssd_conv_w, scb, dtb, alog, dskip, sng)
        y_d = _lru(h, l, win_d, lru_conv_w, lcb, wai, bai, lam)
        ys = [y.reshape(t, MIX) for y in (y_a, y_b, y_c, y_d)]
        x2 = _merge(x2, h2, l, wg, ys, wb, wo)
        last = l == depth - 1
        g_next = g_final.reshape(1, 1, d) if last else gm[l + 1:l + 2]
        outs = _mlp_ple(x2, l, gmlp, w1, w2, gple, wpg, p3, wple, g_next, last)
        if last:
            x2 = outs[0]
        else:
            x2, h2 = outs
    return x2.reshape(b, s, d)
```

```python
import functools

import jax
import jax.numpy as jnp
import numpy as np
from jax import lax
from jax.experimental import pallas as pl
from jax.experimental.pallas import tpu as pltpu

F32 = jnp.float32
BF16 = jnp.bfloat16

D_MODEL = 1024
MIX = 512
N_BRANCH = 4
HEADS = 8
QK_NOPE = 64
QK_ROPE = 32
V_HEAD = 64
Q_LORA = 384
KV_LORA = 256
ROPE_THETA = 10000.0
POOL_WINDOWS = (2, 4, 8, 16)
POOL_GROUP = 128
POOL_HALO = 16
SSD_HEADS = 8
SSD_HEADDIM = 64
SSD_GROUPS = 2
SSD_STATE = 64
SSD_CHUNK = 128
CONV_WIDTH = 4
CONV_HALO = 8
SSD_XBC = 768
LRU_BLOCKS = 8
LRU_BLOCK = 64
LRU_C = 8.0
D_FF = 4096
PLE_DIM = 256
EPS = 1e-6
LANES = 128
SUBLANES = 8
HEAD_PAD = 128
NEG_BIG = -1e30
LOG2_E = 1.4426950408889634
VMEM_LIMIT = 56 * 1024 * 1024

SEQ_TILE = 512
POOL_SUB = 4
SSD_SUB = 2
LRU_SUB = 4
NORM_TILE = 2048
TOK_TILE = 1024
ATT_TILE = 512
ATT_HEADS = 4
ATT_SKEW = 1


def _dot(a, b):
    return jnp.dot(a, b, preferred_element_type=F32)


def _dot_nt(a, b):
    return lax.dot_general(a, b, (((1,), (1,)), ((), ())), preferred_element_type=F32)


def _dot_tn(a, b):
    return lax.dot_general(a, b, (((0,), (0,)), ((), ())), preferred_element_type=F32)


def _split3(a):
    hi = a.astype(BF16)
    r1 = a - hi.astype(F32)
    mid = r1.astype(BF16)
    lo = (r1 - mid.astype(F32)).astype(BF16)
    return hi, mid, lo


def _sel_dot2(a, sel):
    hi = a.astype(BF16)
    mid = (a - hi.astype(F32)).astype(BF16)
    return _dot(hi, sel) + _dot(mid, sel)


def _rmsnorm(x, g):
    return x * lax.rsqrt(jnp.mean(x * x, axis=-1, keepdims=True) + EPS) * g


def _sigmoid(x):
    return jax.nn.sigmoid(x)


def _silu(x):
    return x * jax.nn.sigmoid(x)


def _softplus(x):
    return jnp.maximum(x, 0.0) + jnp.log1p(jnp.exp(-jnp.abs(x)))


def _gelu_tanh(x):
    return 0.5 * x * (1.0 + jnp.tanh(0.7978845608028654 * (x + 0.044715 * (x * x * x))))


def _params(*sem):
    return pltpu.CompilerParams(dimension_semantics=sem, vmem_limit_bytes=VMEM_LIMIT)


def _full(shape):
    n = len(shape)
    return pl.BlockSpec(shape, lambda *_: (0,) * n)


def _layer(stack, l, resident=False):
    n = stack.ndim - 1
    mode = dict(pipeline_mode=pl.Buffered(1)) if resident else {}
    return pl.BlockSpec((None,) + tuple(stack.shape[1:]), lambda *_: (l,) + (0,) * n, **mode)


def _prenorm_kernel(x_ref, g_ref, h_ref):
    h_ref[...] = _rmsnorm(x_ref[...], g_ref[...]).astype(BF16)


def _prenorm(x2, l, g):
    t, d = x2.shape
    tm = min(NORM_TILE, t)
    row = pl.BlockSpec((tm, d), lambda i: (i, 0))
    return pl.pallas_call(
        _prenorm_kernel,
        grid=(t // tm,),
        in_specs=[row, _layer(g, l)],
        out_specs=row,
        out_shape=jax.ShapeDtypeStruct((t, d), BF16),
        compiler_params=_params("arbitrary"),
        name="prenorm",
    )(x2, g)


ROPE_HALF = QK_ROPE // 2
ROPE_PACK = LANES // ROPE_HALF


def _rope_kernel(ang_ref, place_ref, cos_ref, sin_ref, *, blk):
    a = ang_ref[...]
    cos_parts = _split3(jnp.cos(a))
    sin_parts = _split3(jnp.sin(a))
    lane = lax.broadcasted_iota(jnp.int32, (1, LANES), 1)
    ones = (lane < QK_NOPE).astype(F32)
    for i in range(ROPE_PACK):
        ci = sum(_dot(part, place_ref[0, i]) for part in cos_parts) + ones
        si = sum(_dot(part, place_ref[1, i]) for part in sin_parts)
        cos_ref[pl.ds(i, blk, stride=ROPE_PACK), :] = ci
        sin_ref[pl.ds(i, blk, stride=ROPE_PACK), :] = si


def _rope_placement():
    place = np.zeros((2, ROPE_PACK, LANES, LANES), np.float32)
    for i in range(ROPE_PACK):
        for j in range(ROPE_HALF):
            src = i * ROPE_HALF + j
            place[0, i, src, QK_NOPE + j] = 1.0
            place[0, i, src, QK_NOPE + ROPE_HALF + j] = 1.0
            place[1, i, src, QK_NOPE + j] = -1.0
            place[1, i, src, QK_NOPE + ROPE_HALF + j] = 1.0
    return jnp.asarray(place, BF16)


def _rope_tables(positions):
    b, s = positions.shape
    t = b * s
    inv = 1.0 / (ROPE_THETA ** (jnp.arange(0, QK_ROPE, 2, dtype=F32) / QK_ROPE))
    rows = t // ROPE_PACK
    pos = positions.astype(F32).reshape(rows, ROPE_PACK)
    ang = jnp.repeat(pos, ROPE_HALF, axis=1) * jnp.tile(inv, ROPE_PACK)[None, :]
    blk = min(rows, 512)
    place = _rope_placement()
    out_spec = pl.BlockSpec((blk * ROPE_PACK, LANES), lambda i: (i, 0))
    cosq, sinq = pl.pallas_call(
        functools.partial(_rope_kernel, blk=blk),
        grid=(rows // blk,),
        in_specs=[pl.BlockSpec((blk, LANES), lambda i: (i, 0)), _full(place.shape)],
        out_specs=[out_spec, out_spec],
        out_shape=[jax.ShapeDtypeStruct((t, LANES), F32)] * 2,
        compiler_params=_params("arbitrary"),
        name="rope_tables",
    )(ang, place)
    return cosq.reshape(b, s, LANES), sinq.reshape(b, s, LANES)


MLA_IN_COLS = Q_LORA + KV_LORA + 2 * HEAD_PAD


def _mla_proj_kernel(h_ref, win_ref, qn_ref, wq_ref, kvn_ref, wk_ref, wv_ref, cos_ref, sin_ref,
                     q_out, k_out, v_out):
    scale = (QK_NOPE + QK_ROPE) ** -0.5 * LOG2_E
    u = _dot(h_ref[0], win_ref[...])
    c_q = u[:, :Q_LORA]
    c_kv = u[:, Q_LORA:Q_LORA + KV_LORA]
    kr = u[:, Q_LORA + KV_LORA:Q_LORA + KV_LORA + HEAD_PAD]
    kr_sw = u[:, Q_LORA + KV_LORA + HEAD_PAD:]
    cosq = cos_ref[0]
    sinq = sin_ref[0]
    cqn = _rmsnorm(c_q, qn_ref[...]).astype(BF16)
    qq = _dot(cqn, wq_ref[...])
    ckvn = _rmsnorm(c_kv, kvn_ref[...]).astype(BF16)
    kn = _dot(ckvn, wk_ref[...])
    v = _dot(ckvn, wv_ref[...])
    lane = lax.broadcasted_iota(jnp.int32, (1, HEAD_PAD), 1)
    ones_lane = (lane == V_HEAD).astype(F32)
    k_rope = kr * cosq + kr_sw * sinq
    wide = HEADS * HEAD_PAD
    for hd in range(HEADS):
        lo = hd * HEAD_PAD
        q_h = (qq[:, lo:lo + HEAD_PAD] * cosq + qq[:, wide + lo:wide + lo + HEAD_PAD] * sinq) * scale
        q_out[0, hd] = q_h.astype(BF16)
        k_out[0, hd] = (kn[:, lo:lo + HEAD_PAD] + k_rope).astype(BF16)
        v_out[0, hd] = (v[:, lo:lo + HEAD_PAD] + ones_lane).astype(BF16)


def _mla_weights(w_in, w_uq, w_ukv):
    depth = w_in.shape[0]
    half = QK_ROPE // 2
    pad_l, pad_r = QK_NOPE, HEAD_PAD - QK_NOPE - QK_ROPE
    o = Q_LORA + KV_LORA
    w_kr = w_in[..., o:o + QK_ROPE]
    w_kr_sw = jnp.concatenate([w_kr[..., half:], w_kr[..., :half]], axis=-1)
    place = lambda w: jnp.pad(w, ((0, 0), (0, 0), (pad_l, pad_r)))
    win = jnp.concatenate([w_in[..., :o], place(w_kr), place(w_kr_sw)], axis=-1).astype(BF16)
    wq4 = w_uq.reshape(depth, Q_LORA, HEADS, QK_NOPE + QK_ROPE)
    wq_a = jnp.pad(wq4, ((0, 0), (0, 0), (0, 0), (0, pad_r))).reshape(depth, Q_LORA, HEADS * HEAD_PAD)
    rope = wq4[..., QK_NOPE:]
    rope_sw = jnp.concatenate([rope[..., half:], rope[..., :half]], axis=-1)
    wq_b = jnp.pad(rope_sw, ((0, 0), (0, 0), (0, 0), (pad_l, pad_r))).reshape(depth, Q_LORA, HEADS * HEAD_PAD)
    wq = jnp.concatenate([wq_a, wq_b], axis=-1).astype(BF16)
    wkv4 = w_ukv.reshape(depth, KV_LORA, HEADS, QK_NOPE + V_HEAD)
    pad_head = lambda w, n: jnp.pad(w, ((0, 0), (0, 0), (0, 0), (0, HEAD_PAD - n))).reshape(
        depth, KV_LORA, HEADS * HEAD_PAD).astype(BF16)
    return win, wq, pad_head(wkv4[..., :QK_NOPE], QK_NOPE), pad_head(wkv4[..., QK_NOPE:], V_HEAD)


def _mla_proj(h, l, win, qn, wq, kvn, wk, wv, cosq, sinq):
    b, s, d = h.shape
    ts = min(TOK_TILE, s)
    tile = lambda w: pl.BlockSpec((1, ts, w), lambda i, j: (i, j, 0))
    head_out = lambda n: pl.BlockSpec((1, n, ts, HEAD_PAD), lambda i, j: (i, 0, j, 0))
    return pl.pallas_call(
        _mla_proj_kernel,
        grid=(b, s // ts),
        in_specs=[tile(d), _layer(win, l), _layer(qn, l), _layer(wq, l), _layer(kvn, l), _layer(wk, l),
                  _layer(wv, l), tile(HEAD_PAD), tile(HEAD_PAD)],
        out_specs=[head_out(HEADS)] * 3,
        out_shape=[jax.ShapeDtypeStruct((b, HEADS, s, HEAD_PAD), BF16)] * 3,
        compiler_params=_params("arbitrary", "arbitrary"),
        name="mla_proj",
    )(h, win, qn, wq, kvn, wk, wv, cosq, sinq)


def _attn_kernel(q_ref, k_ref, v_ref, o_ref, sa_ref, sb_ref, m_ref, acc_ref, *, tq, nq):
    half = tq // 2
    lane = lax.broadcasted_iota(jnp.int32, (tq, LANES), 1)
    diag_parts = []
    for r0, keys in ((0, half), (half, tq)):
        row = lax.broadcasted_iota(jnp.int32, (half, keys), 0) + r0
        col = lax.broadcasted_iota(jnp.int32, (half, keys), 1)
        diag_parts.append((r0, keys, row >= col))

    def produce(dst_ref, hh, qi, kb, diag):
        q0 = pl.multiple_of(qi * tq, tq)
        k0 = pl.multiple_of(kb * tq, tq)
        parts = [(r0, keys) for r0, keys, _ in diag_parts] if diag else [(0, tq)]
        for r0, keys in parts:
            rows = tq if not diag else half
            dst_ref[hh, r0:r0 + rows, 0:keys] = _dot_nt(q_ref[0, hh, pl.ds(q0 + r0, rows), :],
                                                        k_ref[0, hh, pl.ds(k0, keys), :])

    def softmax_pv(src_ref, hh, k0, r0, rows, keys, mask):
        sc = src_ref[hh, r0:r0 + rows, 0:keys]
        if mask is not None:
            sc = jnp.where(mask, sc, NEG_BIG)
        m = m_ref[hh, r0:r0 + rows, :]
        m_new = jnp.maximum(m, jnp.max(sc, axis=-1, keepdims=True))
        p = jnp.exp2(sc - jnp.tile(m_new, (1, keys // LANES))).astype(BF16)
        acc_ref[hh, r0:r0 + rows, :] = (jnp.exp2(m - m_new) * acc_ref[hh, r0:r0 + rows, :]
                                        + _dot(p, v_ref[0, hh, pl.ds(k0, keys), :]))
        m_ref[hh, r0:r0 + rows, :] = m_new

    def consume(src_ref, hh, kb, diag):
        k0 = pl.multiple_of(kb * tq, tq)
        if diag:
            for r0, keys, mask in diag_parts:
                softmax_pv(src_ref, hh, k0, r0, half, keys, mask)
        else:
            softmax_pv(src_ref, hh, k0, 0, tq, tq, None)

    def step(src_ref, dst_ref, kb, diag, nqi, nkb, next_diag):
        for hh in range(min(ATT_SKEW, ATT_HEADS)):
            produce(dst_ref, hh, nqi, nkb, next_diag)
        for hh in range(ATT_HEADS):
            consume(src_ref, hh, kb, diag)
            if hh + ATT_SKEW < ATT_HEADS:
                produce(dst_ref, hh + ATT_SKEW, nqi, nkb, next_diag)

    def reset():
        m_ref[...] = jnp.full(m_ref.shape, NEG_BIG, F32)
        acc_ref[...] = jnp.zeros(acc_ref.shape, F32)

    def finalize(qi):
        q0 = pl.multiple_of(qi * tq, tq)
        for pair in range(ATT_HEADS // 2):
            acc0 = acc_ref[2 * pair]
            acc1 = acc_ref[2 * pair + 1]
            o0 = acc0 / acc0[:, V_HEAD:V_HEAD + 1]
            o1 = acc1 / acc1[:, V_HEAD:V_HEAD + 1]
            o_ref[0, pl.ds(q0, tq), pair * LANES:(pair + 1) * LANES] = jnp.where(
                lane < V_HEAD, o0, pltpu.roll(o1, V_HEAD, 1)).astype(BF16)

    n_steps = nq * (nq + 1) // 2
    bufs = (sa_ref, sb_ref)
    reset()
    for hh in range(ATT_HEADS):
        produce(sa_ref, hh, 0, 0, True)

    def body(t, carry):
        qi, kb = carry
        diag = kb == qi
        next_diag = kb + 1 == qi
        nqi = jnp.where(diag, qi + 1, qi)
        nkb = jnp.where(diag, 0, kb + 1)
        for par in range(2):
            src, dst = bufs[par], bufs[1 - par]
            here = t % 2 == par

            @pl.when(here & jnp.logical_not(diag) & jnp.logical_not(next_diag))
            def _():
                step(src, dst, kb, False, nqi, nkb, False)

            @pl.when(here & next_diag)
            def _():
                step(src, dst, kb, False, nqi, nkb, True)

            @pl.when(here & diag)
            def _():
                step(src, dst, kb, True, nqi, nkb, False)
                finalize(qi)
                reset()

        return nqi, nkb

    lax.fori_loop(0, n_steps - 1, body, (jnp.int32(0), jnp.int32(0)))
    for hh in range(ATT_HEADS):
        consume(bufs[(n_steps - 1) % 2], hh, nq - 1, True)
    finalize(nq - 1)


def _attention(q, k, v):
    b, _, s, _ = q.shape
    tq = min(ATT_TILE, s)
    spec = pl.BlockSpec((1, ATT_HEADS, s, HEAD_PAD), lambda i, hg: (i, hg, 0, 0))
    return pl.pallas_call(
        functools.partial(_attn_kernel, tq=tq, nq=s // tq),
        grid=(b, HEADS // ATT_HEADS),
        in_specs=[spec, spec, spec],
        out_specs=pl.BlockSpec((1, s, ATT_HEADS * V_HEAD), lambda i, hg: (i, 0, hg)),
        out_shape=jax.ShapeDtypeStruct((b, s, MIX), BF16),
        scratch_shapes=[pltpu.VMEM((ATT_HEADS, tq, tq), F32), pltpu.VMEM((ATT_HEADS, tq, tq), F32),
                        pltpu.VMEM((ATT_HEADS, tq, LANES), F32), pltpu.VMEM((ATT_HEADS, tq, HEAD_PAD), F32)],
        compiler_params=_params("arbitrary", "arbitrary"),
        name="mla_attention",
    )(q, k, v)


def _sub_tiles(seq_len, ts, want):
    n = seq_len // ts
    return max(k for k in range(1, want + 1) if n % k == 0)


def _run_ahead(sub, project, work):
    project(0)
    for k in range(sub):
        if k + 1 < sub:
            project(k + 1)
        work(k)


def _pool_kernel(h_ref, win_ref, wp_ref, ps_ref, o_ref, ext_ref, y_ref, *, ts, sub):
    si = pl.program_id(1)

    @pl.when(si == 0)
    def _():
        ext_ref[0:POOL_HALO, :] = jnp.zeros((POOL_HALO, MIX), F32)

    def project(k):
        r0 = POOL_HALO + k * ts
        ext_ref[r0:r0 + ts, :] = _dot(h_ref[0, k * ts:(k + 1) * ts, :], win_ref[...])

    def work(k):
        r0 = POOL_HALO + k * ts
        t = (si * sub + k) * ts + lax.broadcasted_iota(jnp.int32, (ts, 1), 0)
        for g, w in enumerate(POOL_WINDOWS):
            lo = g * POOL_GROUP
            win = ext_ref[r0 - POOL_HALO:r0 + ts, lo:lo + POOL_GROUP]
            u_g = win[POOL_HALO:]
            shift = 1
            while shift < w:
                win = win + pltpu.roll(win, shift, 0)
                shift *= 2
            win = win[POOL_HALO:]
            count = jnp.minimum(t + 1, w).astype(F32)
            d = win / count - u_g
            y = _dot(d.astype(BF16), wp_ref[g])
            y_ref[k * ts:(k + 1) * ts, lo:lo + POOL_GROUP] = (y * ps_ref[:, lo:lo + POOL_GROUP]).astype(BF16)

    _run_ahead(sub, project, work)
    ext_ref[0:POOL_HALO, :] = ext_ref[sub * ts:sub * ts + POOL_HALO, :]
    o_ref[0] = y_ref[...]


def _pool(h, l, win, wp, ps):
    b, s, d = h.shape
    ts = min(SEQ_TILE, s)
    sub = _sub_tiles(s, ts, POOL_SUB)
    rows = sub * ts
    return pl.pallas_call(
        functools.partial(_pool_kernel, ts=ts, sub=sub),
        grid=(b, s // rows),
        in_specs=[pl.BlockSpec((1, rows, d), lambda i, j: (i, j, 0)), _layer(win, l), _layer(wp, l), _layer(ps, l)],
        out_specs=pl.BlockSpec((1, rows, MIX), lambda i, j: (i, j, 0)),
        out_shape=jax.ShapeDtypeStruct((b, s, MIX), BF16),
        scratch_shapes=[pltpu.VMEM((POOL_HALO + rows, MIX), F32), pltpu.VMEM((rows, MIX), BF16)],
        compiler_params=_params("arbitrary", "arbitrary"),
        name="pool_mixer",
    )(h, win, wp, ps)


def _causal_conv(buf_ref, r0, c0, width, cw_ref, cb_ref, ts):
    def rows(back):
        return buf_ref[r0 - back:r0 - back + ts, c0:c0 + width]

    acc = rows(0) * cw_ref[CONV_WIDTH - 1:CONV_WIDTH, :] + cb_ref[...]
    for jj in range(CONV_WIDTH - 1):
        acc = acc + rows(CONV_WIDTH - 1 - jj) * cw_ref[jj:jj + 1, :]
    return acc


SSD_IN_COLS = MIX + SSD_XBC + LANES
SSD_GROUP_COLS = MIX // SSD_GROUPS
SSD_GROUP_HEADS = SSD_HEADS // SSD_GROUPS


def _ssd_kernel(h_ref, win_ref, cw_ref, cb_ref, dtb_ref, alog_ref, dskip_ref, ng_ref,
                tri_ref, expand_ref, bd_ref, o_ref, u_ref, xbc_ref, y_ref, state_ref, *, ts, sub):
    si = pl.program_id(1)
    lc = SSD_CHUNK
    row = lax.broadcasted_iota(jnp.int32, (lc, lc), 0)
    col = lax.broadcasted_iota(jnp.int32, (lc, lc), 1)
    tril = row >= col
    lane = lax.broadcasted_iota(jnp.int32, (1, LANES), 1)

    @pl.when(si == 0)
    def _():
        state_ref[...] = jnp.zeros(state_ref.shape, F32)
        u_ref[0:CONV_HALO, :] = jnp.zeros((CONV_HALO, SSD_IN_COLS), F32)

    def project(k):
        r0 = CONV_HALO + k * ts
        u_ref[r0:r0 + ts, :] = _dot(h_ref[0, k * ts:(k + 1) * ts, :], win_ref[...])

    def work(k):
        r0 = CONV_HALO + k * ts
        xbc_ref[k] = _silu(_causal_conv(u_ref, r0, MIX, SSD_XBC, cw_ref, cb_ref, ts))
        dt_all = _softplus(u_ref[r0:r0 + ts, MIX + SSD_XBC:] + dtb_ref[...])
        a_all = dt_all * (-jnp.exp(alog_ref[...]))
        ys = []
        for c in range(ts // lc):
            c0 = c * lc
            xs = xbc_ref[k, c0:c0 + lc, 0:MIX]
            bm = xbc_ref[k, c0:c0 + lc, MIX:MIX + LANES]
            cm = xbc_ref[k, c0:c0 + lc, MIX + LANES:MIX + 2 * LANES]
            dt = dt_all[c0:c0 + lc]
            cs = _sel_dot_left(tri_ref[...], a_all[c0:c0 + lc])
            cs_t = cs.T
            dt_t = dt.T
            cs_end = cs[lc - 1:lc, :]
            per_head = jnp.concatenate([jnp.exp(cs), dt * jnp.exp(cs_end - cs)], axis=0)
            wide = _sel_dot2(per_head, expand_ref[...])
            ecs_x, dd_x = wide[0:lc], wide[lc:2 * lc]
            xs_b = xs.astype(BF16)
            xdd = (xs * dd_x).astype(BF16)
            y_parts = []
            for grp in range(SSD_GROUPS):
                in_grp = (lane >= grp * SSD_STATE) & (lane < (grp + 1) * SSD_STATE)
                cm_g = jnp.where(in_grp, cm, 0.0).astype(BF16)
                bm_g = jnp.where(in_grp, bm, 0.0).astype(BF16)
                cb = _dot_nt(cm_g, bm_g)
                gc = grp * SSD_GROUP_COLS
                masks = []
                for hl in range(SSD_GROUP_HEADS):
                    hd = grp * SSD_GROUP_HEADS + hl
                    seg = jnp.where(tril, cs[:, hd:hd + 1] - cs_t[hd:hd + 1, :], NEG_BIG)
                    masks.append((cb * jnp.exp(seg) * dt_t[hd:hd + 1, :]).astype(BF16))
                m_cat = jnp.concatenate(masks, axis=1)
                x_bd = jnp.tile(xs_b[:, gc:gc + SSD_GROUP_COLS], (SSD_GROUP_HEADS, 1)) * bd_ref[...]
                y_diag = _dot(m_cat, x_bd)
                st = state_ref[grp]
                y_off = _dot(cm_g, st.astype(BF16)) * ecs_x[:, gc:gc + SSD_GROUP_COLS]
                state_ref[grp] = (st * ecs_x[lc - 1:lc, gc:gc + SSD_GROUP_COLS]
                                  + _dot_tn(bm_g, xdd[:, gc:gc + SSD_GROUP_COLS]))
                y_parts.append(y_diag + y_off)
            ys.append(jnp.concatenate(y_parts, axis=1) + xs * dskip_ref[...])
        y = jnp.concatenate(ys, axis=0) * _silu(u_ref[r0:r0 + ts, 0:MIX])
        y_ref[k * ts:(k + 1) * ts, :] = _rmsnorm(y, ng_ref[...]).astype(BF16)

    _run_ahead(sub, project, work)
    u_ref[0:CONV_HALO, :] = u_ref[sub * ts:sub * ts + CONV_HALO, :]
    o_ref[0] = y_ref[...]


def _sel_dot_left(sel, a):
    hi, mid, lo = _split3(a)
    return _dot(sel, hi) + _dot(sel, mid) + _dot(sel, lo)


def _ssd_constants():
    lc = SSD_CHUNK
    tri = (jnp.arange(lc)[:, None] >= jnp.arange(lc)[None, :]).astype(BF16)
    expand = (jnp.arange(LANES)[:, None] == (jnp.arange(MIX)[None, :] // SSD_HEADDIM)).astype(BF16)
    rows = jnp.arange(SSD_GROUP_HEADS * lc)[:, None] // lc
    cols = jnp.arange(SSD_GROUP_COLS)[None, :] // SSD_HEADDIM
    bd = (rows == cols).astype(BF16)
    return tri, expand, bd


def _ssd(h, l, win, cw, cb, dtb, alog, dskip, ng):
    b, s, d = h.shape
    ts = min(SEQ_TILE, s)
    sub = _sub_tiles(s, ts, SSD_SUB)
    rows = sub * ts
    tri, expand, bd = _ssd_constants()
    return pl.pallas_call(
        functools.partial(_ssd_kernel, ts=ts, sub=sub),
        grid=(b, s // rows),
        in_specs=[pl.BlockSpec((1, rows, d), lambda i, j: (i, j, 0)), _layer(win, l), _layer(cw, l), _layer(cb, l),
                  _layer(dtb, l), _layer(alog, l), _layer(dskip, l), _layer(ng, l),
                  _full(tri.shape), _full(expand.shape), _full(bd.shape)],
        out_specs=pl.BlockSpec((1, rows, MIX), lambda i, j: (i, j, 0)),
        out_shape=jax.ShapeDtypeStruct((b, s, MIX), BF16),
        scratch_shapes=[pltpu.VMEM((CONV_HALO + rows, SSD_IN_COLS), F32),
                        pltpu.VMEM((sub, ts, SSD_XBC), F32),
                        pltpu.VMEM((rows, MIX), BF16),
                        pltpu.VMEM((SSD_GROUPS, LANES, SSD_GROUP_COLS), F32)],
        compiler_params=_params("arbitrary", "arbitrary"),
        name="ssd_mixer",
    )(h, win, cw, cb, dtb, alog, dskip, ng, tri, expand, bd)


def _lru_kernel(hn_ref, win_ref, cw_ref, cb_ref, wai_ref, bai_ref, lam_ref, o_ref,
                u_ref, a_ref, h_ref, y_ref, carry_ref, *, ts, sub):
    si = pl.program_id(1)
    rowi = lax.broadcasted_iota(jnp.int32, (SUBLANES, MIX), 0)

    @pl.when(si == 0)
    def _():
        carry_ref[...] = jnp.zeros(carry_ref.shape, F32)
        u_ref[0:CONV_HALO, :] = jnp.zeros((CONV_HALO, 2 * MIX), F32)

    def project(k):
        r0 = CONV_HALO + k * ts
        u_ref[r0:r0 + ts, :] = _dot(hn_ref[0, k * ts:(k + 1) * ts, :], win_ref[...])

    def work(k):
        r0 = CONV_HALO + k * ts
        xc = _causal_conv(u_ref, r0, MIX, MIX, cw_ref, cb_ref, ts)
        ri = _dot(xc.astype(BF16), wai_ref[...]) + bai_ref[...]
        r_t = _sigmoid(ri[:, :MIX])
        i_t = _sigmoid(ri[:, MIX:])
        log_a = (-LRU_C) * r_t * _softplus(-lam_ref[...])
        a_t = jnp.exp(log_a)
        mult = jnp.sqrt(jnp.tanh(-log_a) * (a_t * a_t + 1.0))
        a_ref[k] = a_t
        h_ref[k] = xc * i_t * mult
        carry = carry_ref[...]
        for blk in range(ts // SUBLANES):
            b0 = blk * SUBLANES
            a8 = a_ref[k, b0:b0 + SUBLANES, :]
            u8 = h_ref[k, b0:b0 + SUBLANES, :]
            for kk in (1, 2, 4):
                keep = rowi >= kk
                u8 = jnp.where(keep, a8 * pltpu.roll(u8, kk, 0) + u8, u8)
                a8 = jnp.where(keep, a8 * pltpu.roll(a8, kk, 0), a8)
            h8 = u8 + a8 * carry
            h_ref[k, b0:b0 + SUBLANES, :] = h8
            carry = h8[SUBLANES - 1:SUBLANES, :]
        carry_ref[...] = carry
        y_ref[k * ts:(k + 1) * ts, :] = (h_ref[k] * _gelu_tanh(u_ref[r0:r0 + ts, 0:MIX])).astype(BF16)

    _run_ahead(sub, project, work)
    u_ref[0:CONV_HALO, :] = u_ref[sub * ts:sub * ts + CONV_HALO, :]
    o_ref[0] = y_ref[...]


def _block_diag(w):
    depth, nb, n, _ = w.shape
    eye = jnp.eye(nb, dtype=w.dtype)
    return (eye[:, None, :, None] * w[:, :, :, None, :]).reshape(depth, nb * n, nb * n)


def _lru(h, l, win, cw, cb, wai, bai, lam):
    b, s, d = h.shape
    ts = min(SEQ_TILE, s)
    sub = _sub_tiles(s, ts, LRU_SUB)
    rows = sub * ts
    return pl.pallas_call(
        functools.partial(_lru_kernel, ts=ts, sub=sub),
        grid=(b, s // rows),
        in_specs=[pl.BlockSpec((1, rows, d), lambda i, j: (i, j, 0)), _layer(win, l), _layer(cw, l), _layer(cb, l),
                  _layer(wai, l), _layer(bai, l), _layer(lam, l)],
        out_specs=pl.BlockSpec((1, rows, MIX), lambda i, j: (i, j, 0)),
        out_shape=jax.ShapeDtypeStruct((b, s, MIX), BF16),
        scratch_shapes=[pltpu.VMEM((CONV_HALO + rows, 2 * MIX), F32), pltpu.VMEM((sub, ts, MIX), F32),
                        pltpu.VMEM((sub, ts, MIX), F32), pltpu.VMEM((rows, MIX), BF16), pltpu.VMEM((1, MIX), F32)],
        compiler_params=_params("arbitrary", "arbitrary"),
        name="rglru_mixer",
    )(h, win, cw, cb, wai, bai, lam)


def _merge_kernel(x_ref, h_ref, wg_ref, ya_ref, yb_ref, yc_ref, yd_ref, wb_ref, wo_ref, o_ref):
    h = h_ref[...]
    merged = None
    for n, y_ref in enumerate((ya_ref, yb_ref, yc_ref, yd_ref)):
        gate = _sigmoid(_dot(h, wg_ref[:, n * D_MODEL:(n + 1) * D_MODEL]))
        term = gate * _dot(y_ref[...], wb_ref[n])
        merged = term if merged is None else merged + term
    o_ref[...] = x_ref[...] + _dot(merged.astype(BF16), wo_ref[...])


def _merge(x2, h2, l, wg, ys, wb, wo):
    t, d = x2.shape
    tm = min(TOK_TILE, t)
    row = lambda w: pl.BlockSpec((tm, w), lambda i: (i, 0))
    return pl.pallas_call(
        _merge_kernel,
        grid=(t // tm,),
        in_specs=[row(d), row(d), _layer(wg, l), row(MIX), row(MIX), row(MIX), row(MIX),
                  _layer(wb, l), _layer(wo, l)],
        out_specs=row(d),
        out_shape=jax.ShapeDtypeStruct((t, d), F32),
        compiler_params=_params("arbitrary"),
        name="gated_merge",
    )(x2, h2, wg, *ys, wb, wo)


FF_CHUNK = 1024


def _mlp_ple_kernel(x_ref, g_ref, w1_ref, w2_ref, gp_ref, wpg_ref, p_ref, wple_ref, gn_ref, *out_refs, last):
    x = x_ref[...]
    h = _rmsnorm(x, g_ref[...]).astype(BF16)
    acc = x
    for c in range(D_FF // FF_CHUNK):
        hid = jnp.maximum(_dot(h, w1_ref[:, c * FF_CHUNK:(c + 1) * FF_CHUNK]), 0.0)
        acc = acc + _dot((hid * hid).astype(BF16), w2_ref[c * FF_CHUNK:(c + 1) * FF_CHUNK, :])
    gate = _sigmoid(_dot(_rmsnorm(acc, gp_ref[...]).astype(BF16), wpg_ref[...]))
    y = acc + _dot(p_ref[0].astype(BF16), wple_ref[...]) * gate
    normed = _rmsnorm(y, gn_ref[...])
    if last:
        out_refs[0][...] = normed
    else:
        out_refs[0][...] = y
        out_refs[1][...] = normed.astype(BF16)


def _mlp_ple(x2, l, g, w1, w2, gp, wpg, p3, wple, g_next, last):
    t, d = x2.shape
    tm = min(TOK_TILE, t)
    row = lambda w: pl.BlockSpec((tm, w), lambda i: (i, 0))
    out_specs = [row(d)] if last else [row(d), row(d)]
    out_shape = [jax.ShapeDtypeStruct((t, d), F32)] + ([] if last else [jax.ShapeDtypeStruct((t, d), BF16)])
    return pl.pallas_call(
        functools.partial(_mlp_ple_kernel, last=last),
        grid=(t // tm,),
        in_specs=[row(d), _layer(g, l), _layer(w1, l, True), _layer(w2, l, True), _layer(gp, l),
                  _layer(wpg, l, True), pl.BlockSpec((1, tm, PLE_DIM), lambda i: (l, i, 0)),
                  _layer(wple, l, True), _layer(g_next, 0)],
        out_specs=out_specs,
        out_shape=out_shape,
        compiler_params=_params("arbitrary"),
        name="mlp_ple",
    )(x2, g, w1, w2, gp, wpg, p3, wple, g_next)


_SPLIT = (Q_LORA, KV_LORA, QK_ROPE, MIX, MIX, SSD_XBC, SSD_HEADS, MIX, MIX, N_BRANCH * D_MODEL)


def _col_offsets():
    offs, acc = [], 0
    for sz in _SPLIT:
        offs.append(acc)
        acc += sz
    return offs


def kernel(x, p, positions, g_mix, w_in, q_norm, w_uq, kv_norm, w_ukv, w_pool, pool_scale,
           ssd_conv_w, ssd_conv_b, ssd_dt_bias, ssd_a_log, ssd_d, ssd_norm,
           lru_conv_w, lru_conv_b, lru_w_a, lru_b_a, lru_w_i, lru_b_i, lru_lambda,
           w_branch, w_out, g_mlp, w_ff1, w_ff2, g_ple, w_ple_gate, w_ple, g_final):
    b, s, d = x.shape
    depth = w_in.shape[0]
    t = b * s
    offs = _col_offsets()
    o_pool, o_z, o_xbc, o_dt, o_lg, o_lx, o_gate = offs[3], offs[4], offs[5], offs[6], offs[7], offs[8], offs[9]
    cosq, sinq = _rope_tables(positions)
    bf = lambda a: a.astype(BF16)
    vec = lambda a: a.reshape(depth, 1, -1)
    win_a, wq, wk, wv = _mla_weights(w_in, w_uq, w_ukv)
    win_b = bf(w_in[..., o_pool:o_pool + MIX])
    w_dt = jnp.pad(w_in[..., o_dt:o_dt + SSD_HEADS], ((0, 0), (0, 0), (0, LANES - SSD_HEADS)))
    win_c = bf(jnp.concatenate([w_in[..., o_z:o_z + MIX], w_in[..., o_xbc:o_xbc + SSD_XBC], w_dt], axis=-1))
    head_lanes = lambda a: vec(jnp.pad(a, ((0, 0), (0, LANES - SSD_HEADS))))
    win_d = bf(w_in[..., o_lg:o_lg + 2 * MIX])
    wai = bf(jnp.concatenate([_block_diag(lru_w_a), _block_diag(lru_w_i)], axis=-1))
    bai = vec(jnp.concatenate([lru_b_a, lru_b_i], axis=-1))
    wg, wb, wo = bf(w_in[..., o_gate:]), bf(w_branch), bf(w_out)
    w1, w2, wpg, wple = bf(w_ff1), bf(w_ff2), bf(w_ple_gate), bf(w_ple)
    gm, gmlp, gple = vec(g_mix), vec(g_mlp), vec(g_ple)
    qn, kvn, ps = vec(q_norm), vec(kv_norm), vec(pool_scale)
    scb, dtb, alog = vec(ssd_conv_b), head_lanes(ssd_dt_bias), head_lanes(ssd_a_log)
    dskip, sng = vec(jnp.repeat(ssd_d, SSD_HEADDIM, axis=-1)), vec(ssd_norm)
    lcb, lam = vec(lru_conv_b), vec(lru_lambda)
    wp = bf(w_pool)
    p3 = p.reshape(depth, t, PLE_DIM)
    x2 = x.reshape(t, d)
    h2 = _prenorm(x2, 0, gm)
    for l in range(depth):
        h = h2.reshape(b, s, d)
        q, k, v = _mla_proj(h, l, win_a, qn, wq, kvn, wk, wv, cosq, sinq)
        y_a = _attention(q, k, v)
        y_b = _pool(h, l, win_b, wp, ps)
        y_c = _ssd(h, l, win_c, ssd_conv_w, scb, dtb, alog, dskip, sng)
        y_d = _lru(h, l, win_d, lru_conv_w, lcb, wai, bai, lam)
        ys = [y.reshape(t, MIX) for y in (y_a, y_b, y_c, y_d)]
        x2 = _merge(x2, h2, l, wg, ys, wb, wo)
        last = l == depth - 1
        g_next = g_final.reshape(1, 1, d) if last else gm[l + 1:l + 2]
        outs = _mlp_ple(x2, l, gmlp, w1, w2, gple, wpg, p3, wple, g_next, last)
        if last:
            x2 = outs[0]
        else:
            x2, h2 = outs
    return x2.reshape(b, s, d)
```

```python
import functools

import jax
import jax.numpy as jnp
import numpy as np
from jax import lax
from jax.experimental import pallas as pl
from jax.experimental.pallas import tpu as pltpu

F32 = jnp.float32
BF16 = jnp.bfloat16

D_MODEL = 1024
MIX = 512
N_BRANCH = 4
HEADS = 8
QK_NOPE = 64
QK_ROPE = 32
V_HEAD = 64
Q_LORA = 384
KV_LORA = 256
ROPE_THETA = 10000.0
POOL_WINDOWS = (2, 4, 8, 16)
POOL_GROUP = 128
POOL_HALO = 16
SSD_HEADS = 8
SSD_HEADDIM = 64
SSD_GROUPS = 2
SSD_STATE = 64
SSD_CHUNK = 128
CONV_WIDTH = 4
CONV_HALO = 8
SSD_XBC = 768
LRU_BLOCKS = 8
LRU_BLOCK = 64
LRU_C = 8.0
D_FF = 4096
PLE_DIM = 256
EPS = 1e-6
LANES = 128
SUBLANES = 8
HEAD_PAD = 128
NEG_BIG = -1e30
LOG2_E = 1.4426950408889634
VMEM_LIMIT = 56 * 1024 * 1024

SEQ_TILE = 512
POOL_SUB = 4
SSD_SUB = 2
LRU_SUB = 4
TOK_TILE = 1024
ATT_TILE = 512
ATT_HEADS = 4
ATT_SKEW = 1


def _dot(a, b):
    return jnp.dot(a, b, preferred_element_type=F32)


def _dot_nt(a, b):
    return lax.dot_general(a, b, (((1,), (1,)), ((), ())), preferred_element_type=F32)


def _dot_tn(a, b):
    return lax.dot_general(a, b, (((0,), (0,)), ((), ())), preferred_element_type=F32)


def _split3(a):
    hi = a.astype(BF16)
    r1 = a - hi.astype(F32)
    mid = r1.astype(BF16)
    lo = (r1 - mid.astype(F32)).astype(BF16)
    return hi, mid, lo


def _sel_dot2(a, sel):
    hi = a.astype(BF16)
    mid = (a - hi.astype(F32)).astype(BF16)
    return _dot(hi, sel) + _dot(mid, sel)


def _rmsnorm(x, g):
    return x * lax.rsqrt(jnp.mean(x * x, axis=-1, keepdims=True) + EPS) * g


def _sigmoid(x):
    return jax.nn.sigmoid(x)


def _silu(x):
    return x * jax.nn.sigmoid(x)


def _softplus(x):
    return jnp.maximum(x, 0.0) + jnp.log1p(jnp.exp(-jnp.abs(x)))


def _gelu_tanh(x):
    return 0.5 * x * (1.0 + jnp.tanh(0.7978845608028654 * (x + 0.044715 * (x * x * x))))


def _params(*sem):
    return pltpu.CompilerParams(dimension_semantics=sem, vmem_limit_bytes=VMEM_LIMIT)


def _full(shape):
    n = len(shape)
    return pl.BlockSpec(shape, lambda *_: (0,) * n)


def _layer(stack, l, resident=False):
    n = stack.ndim - 1
    mode = dict(pipeline_mode=pl.Buffered(1)) if resident else {}
    return pl.BlockSpec((None,) + tuple(stack.shape[1:]), lambda *_: (l,) + (0,) * n, **mode)


ROPE_HALF = QK_ROPE // 2
ROPE_PACK = LANES // ROPE_HALF


def _rope_kernel(ang_ref, place_ref, cos_ref, sin_ref, *, blk):
    a = ang_ref[...]
    cos_parts = _split3(jnp.cos(a))
    sin_parts = _split3(jnp.sin(a))
    lane = lax.broadcasted_iota(jnp.int32, (1, LANES), 1)
    ones = (lane < QK_NOPE).astype(F32)
    for i in range(ROPE_PACK):
        ci = sum(_dot(part, place_ref[0, i]) for part in cos_parts) + ones
        si = sum(_dot(part, place_ref[1, i]) for part in sin_parts)
        cos_ref[pl.ds(i, blk, stride=ROPE_PACK), :] = ci
        sin_ref[pl.ds(i, blk, stride=ROPE_PACK), :] = si


def _rope_placement():
    place = np.zeros((2, ROPE_PACK, LANES, LANES), np.float32)
    for i in range(ROPE_PACK):
        for j in range(ROPE_HALF):
            src = i * ROPE_HALF + j
            place[0, i, src, QK_NOPE + j] = 1.0
            place[0, i, src, QK_NOPE + ROPE_HALF + j] = 1.0
            place[1, i, src, QK_NOPE + j] = -1.0
            place[1, i, src, QK_NOPE + ROPE_HALF + j] = 1.0
    return jnp.asarray(place, BF16)


def _rope_tables(positions):
    b, s = positions.shape
    t = b * s
    inv = 1.0 / (ROPE_THETA ** (jnp.arange(0, QK_ROPE, 2, dtype=F32) / QK_ROPE))
    rows = t // ROPE_PACK
    pos = positions.astype(F32).reshape(rows, ROPE_PACK)
    ang = jnp.repeat(pos, ROPE_HALF, axis=1) * jnp.tile(inv, ROPE_PACK)[None, :]
    blk = min(rows, 512)
    place = _rope_placement()
    out_spec = pl.BlockSpec((blk * ROPE_PACK, LANES), lambda i: (i, 0))
    cosq, sinq = pl.pallas_call(
        functools.partial(_rope_kernel, blk=blk),
        grid=(rows // blk,),
        in_specs=[pl.BlockSpec((blk, LANES), lambda i: (i, 0)), _full(place.shape)],
        out_specs=[out_spec, out_spec],
        out_shape=[jax.ShapeDtypeStruct((t, LANES), F32)] * 2,
        compiler_params=_params("arbitrary"),
        name="rope_tables",
    )(ang, place)
    return cosq.reshape(b, s, LANES), sinq.reshape(b, s, LANES)


MLA_IN_COLS = Q_LORA + KV_LORA + 2 * HEAD_PAD


def _mla_proj_kernel(*refs, norm_input):
    if norm_input:
        x_ref, g_ref, *refs = refs
    else:
        h_ref, *refs = refs
    win_ref, qn_ref, wq_ref, kvn_ref, wk_ref, wv_ref, cos_ref, sin_ref, q_out, k_out, v_out, *h_out = refs
    if norm_input:
        h = _rmsnorm(x_ref[0], g_ref[...]).astype(BF16)
        h_out[0][0] = h
    else:
        h = h_ref[0]
    scale = (QK_NOPE + QK_ROPE) ** -0.5 * LOG2_E
    u = _dot(h, win_ref[...])
    c_q = u[:, :Q_LORA]
    c_kv = u[:, Q_LORA:Q_LORA + KV_LORA]
    kr = u[:, Q_LORA + KV_LORA:Q_LORA + KV_LORA + HEAD_PAD]
    kr_sw = u[:, Q_LORA + KV_LORA + HEAD_PAD:]
    cosq = cos_ref[0]
    sinq = sin_ref[0]
    cqn = _rmsnorm(c_q, qn_ref[...]).astype(BF16)
    qq = _dot(cqn, wq_ref[...])
    ckvn = _rmsnorm(c_kv, kvn_ref[...]).astype(BF16)
    kn = _dot(ckvn, wk_ref[...])
    v = _dot(ckvn, wv_ref[...])
    lane = lax.broadcasted_iota(jnp.int32, (1, HEAD_PAD), 1)
    ones_lane = (lane == V_HEAD).astype(F32)
    k_rope = kr * cosq + kr_sw * sinq
    wide = HEADS * HEAD_PAD
    for hd in range(HEADS):
        lo = hd * HEAD_PAD
        q_h = (qq[:, lo:lo + HEAD_PAD] * cosq + qq[:, wide + lo:wide + lo + HEAD_PAD] * sinq) * scale
        q_out[0, hd] = q_h.astype(BF16)
        k_out[0, hd] = (kn[:, lo:lo + HEAD_PAD] + k_rope).astype(BF16)
        v_out[0, hd] = (v[:, lo:lo + HEAD_PAD] + ones_lane).astype(BF16)


def _mla_weights(w_in, w_uq, w_ukv):
    depth = w_in.shape[0]
    half = QK_ROPE // 2
    pad_l, pad_r = QK_NOPE, HEAD_PAD - QK_NOPE - QK_ROPE
    o = Q_LORA + KV_LORA
    w_kr = w_in[..., o:o + QK_ROPE]
    w_kr_sw = jnp.concatenate([w_kr[..., half:], w_kr[..., :half]], axis=-1)
    place = lambda w: jnp.pad(w, ((0, 0), (0, 0), (pad_l, pad_r)))
    win = jnp.concatenate([w_in[..., :o], place(w_kr), place(w_kr_sw)], axis=-1).astype(BF16)
    wq4 = w_uq.reshape(depth, Q_LORA, HEADS, QK_NOPE + QK_ROPE)
    wq_a = jnp.pad(wq4, ((0, 0), (0, 0), (0, 0), (0, pad_r))).reshape(depth, Q_LORA, HEADS * HEAD_PAD)
    rope = wq4[..., QK_NOPE:]
    rope_sw = jnp.concatenate([rope[..., half:], rope[..., :half]], axis=-1)
    wq_b = jnp.pad(rope_sw, ((0, 0), (0, 0), (0, 0), (pad_l, pad_r))).reshape(depth, Q_LORA, HEADS * HEAD_PAD)
    wq = jnp.concatenate([wq_a, wq_b], axis=-1).astype(BF16)
    wkv4 = w_ukv.reshape(depth, KV_LORA, HEADS, QK_NOPE + V_HEAD)
    pad_head = lambda w, n: jnp.pad(w, ((0, 0), (0, 0), (0, 0), (0, HEAD_PAD - n))).reshape(
        depth, KV_LORA, HEADS * HEAD_PAD).astype(BF16)
    return win, wq, pad_head(wkv4[..., :QK_NOPE], QK_NOPE), pad_head(wkv4[..., QK_NOPE:], V_HEAD)


def _mla_proj(stream, gain, l, win, qn, wq, kvn, wk, wv, cosq, sinq):
    b, s, d = stream.shape
    ts = min(TOK_TILE, s)
    tile = lambda w: pl.BlockSpec((1, ts, w), lambda i, j: (i, j, 0))
    head_out = lambda n: pl.BlockSpec((1, n, ts, HEAD_PAD), lambda i, j: (i, 0, j, 0))
    norm_input = gain is not None
    lead = [stream, gain] if norm_input else [stream]
    lead_specs = [tile(d), _layer(gain, l)] if norm_input else [tile(d)]
    return pl.pallas_call(
        functools.partial(_mla_proj_kernel, norm_input=norm_input),
        grid=(b, s // ts),
        in_specs=lead_specs + [_layer(win, l), _layer(qn, l), _layer(wq, l), _layer(kvn, l), _layer(wk, l),
                               _layer(wv, l), tile(HEAD_PAD), tile(HEAD_PAD)],
        out_specs=[head_out(HEADS)] * 3 + ([tile(d)] if norm_input else []),
        out_shape=[jax.ShapeDtypeStruct((b, HEADS, s, HEAD_PAD), BF16)] * 3
        + ([jax.ShapeDtypeStruct((b, s, d), BF16)] if norm_input else []),
        compiler_params=_params("arbitrary", "arbitrary"),
        name="mla_proj",
    )(*lead, win, qn, wq, kvn, wk, wv, cosq, sinq)


def _attn_kernel(q_ref, k_ref, v_ref, o_ref, sa_ref, sb_ref, m_ref, acc_ref, *, tq, nq):
    half = tq // 2
    lane = lax.broadcasted_iota(jnp.int32, (tq, LANES), 1)
    diag_parts = []
    for r0, keys in ((0, half), (half, tq)):
        row = lax.broadcasted_iota(jnp.int32, (half, keys), 0) + r0
        col = lax.broadcasted_iota(jnp.int32, (half, keys), 1)
        diag_parts.append((r0, keys, row >= col))

    def produce(dst_ref, hh, qi, kb, diag):
        q0 = pl.multiple_of(qi * tq, tq)
        k0 = pl.multiple_of(kb * tq, tq)
        parts = [(r0, keys) for r0, keys, _ in diag_parts] if diag else [(0, tq)]
        for r0, keys in parts:
            rows = tq if not diag else half
            dst_ref[hh, r0:r0 + rows, 0:keys] = _dot_nt(q_ref[0, hh, pl.ds(q0 + r0, rows), :],
                                                        k_ref[0, hh, pl.ds(k0, keys), :])

    def softmax_pv(src_ref, hh, k0, r0, rows, keys, mask):
        sc = src_ref[hh, r0:r0 + rows, 0:keys]
        if mask is not None:
            sc = jnp.where(mask, sc, NEG_BIG)
        m = m_ref[hh, r0:r0 + rows, :]
        m_new = jnp.maximum(m, jnp.max(sc, axis=-1, keepdims=True))
        p = jnp.exp2(sc - jnp.tile(m_new, (1, keys // LANES))).astype(BF16)
        acc_ref[hh, r0:r0 + rows, :] = (jnp.exp2(m - m_new) * acc_ref[hh, r0:r0 + rows, :]
                                        + _dot(p, v_ref[0, hh, pl.ds(k0, keys), :]))
        m_ref[hh, r0:r0 + rows, :] = m_new

    def consume(src_ref, hh, kb, diag):
        k0 = pl.multiple_of(kb * tq, tq)
        if diag:
            for r0, keys, mask in diag_parts:
                softmax_pv(src_ref, hh, k0, r0, half, keys, mask)
        else:
            softmax_pv(src_ref, hh, k0, 0, tq, tq, None)

    def step(src_ref, dst_ref, kb, diag, nqi, nkb, next_diag):
        for hh in range(min(ATT_SKEW, ATT_HEADS)):
            produce(dst_ref, hh, nqi, nkb, next_diag)
        for hh in range(ATT_HEADS):
            consume(src_ref, hh, kb, diag)
            if hh + ATT_SKEW < ATT_HEADS:
                produce(dst_ref, hh + ATT_SKEW, nqi, nkb, next_diag)

    def reset():
        m_ref[...] = jnp.full(m_ref.shape, NEG_BIG, F32)
        acc_ref[...] = jnp.zeros(acc_ref.shape, F32)

    def finalize(qi):
        q0 = pl.multiple_of(qi * tq, tq)
        for pair in range(ATT_HEADS // 2):
            acc0 = acc_ref[2 * pair]
            acc1 = acc_ref[2 * pair + 1]
            o0 = acc0 / acc0[:, V_HEAD:V_HEAD + 1]
            o1 = acc1 / acc1[:, V_HEAD:V_HEAD + 1]
            o_ref[0, pl.ds(q0, tq), pair * LANES:(pair + 1) * LANES] = jnp.where(
                lane < V_HEAD, o0, pltpu.roll(o1, V_HEAD, 1)).astype(BF16)

    n_steps = nq * (nq + 1) // 2
    bufs = (sa_ref, sb_ref)
    reset()
    for hh in range(ATT_HEADS):
        produce(sa_ref, hh, 0, 0, True)

    def body(t, carry):
        qi, kb = carry
        diag = kb == qi
        next_diag = kb + 1 == qi
        nqi = jnp.where(diag, qi + 1, qi)
        nkb = jnp.where(diag, 0, kb + 1)
        for par in range(2):
            src, dst = bufs[par], bufs[1 - par]
            here = t % 2 == par

            @pl.when(here & jnp.logical_not(diag) & jnp.logical_not(next_diag))
            def _():
                step(src, dst, kb, False, nqi, nkb, False)

            @pl.when(here & next_diag)
            def _():
                step(src, dst, kb, False, nqi, nkb, True)

            @pl.when(here & diag)
            def _():
                step(src, dst, kb, True, nqi, nkb, False)
                finalize(qi)
                reset()

        return nqi, nkb

    lax.fori_loop(0, n_steps - 1, body, (jnp.int32(0), jnp.int32(0)))
    for hh in range(ATT_HEADS):
        consume(bufs[(n_steps - 1) % 2], hh, nq - 1, True)
    finalize(nq - 1)


def _attention(q, k, v):
    b, _, s, _ = q.shape
    tq = min(ATT_TILE, s)
    spec = pl.BlockSpec((1, ATT_HEADS, s, HEAD_PAD), lambda i, hg: (i, hg, 0, 0))
    return pl.pallas_call(
        functools.partial(_attn_kernel, tq=tq, nq=s // tq),
        grid=(b, HEADS // ATT_HEADS),
        in_specs=[spec, spec, spec],
        out_specs=pl.BlockSpec((1, s, ATT_HEADS * V_HEAD), lambda i, hg: (i, 0, hg)),
        out_shape=jax.ShapeDtypeStruct((b, s, MIX), BF16),
        scratch_shapes=[pltpu.VMEM((ATT_HEADS, tq, tq), F32), pltpu.VMEM((ATT_HEADS, tq, tq), F32),
                        pltpu.VMEM((ATT_HEADS, tq, LANES), F32), pltpu.VMEM((ATT_HEADS, tq, HEAD_PAD), F32)],
        compiler_params=_params("arbitrary", "arbitrary"),
        name="mla_attention",
    )(q, k, v)


def _sub_tiles(seq_len, ts, want):
    n = seq_len // ts
    return max(k for k in range(1, want + 1) if n % k == 0)


def _run_ahead(sub, project, work):
    project(0)
    for k in range(sub):
        if k + 1 < sub:
            project(k + 1)
        work(k)


def _pool_kernel(h_ref, win_ref, wp_ref, ps_ref, o_ref, ext_ref, y_ref, *, ts, sub):
    si = pl.program_id(1)

    @pl.when(si == 0)
    def _():
        ext_ref[0:POOL_HALO, :] = jnp.zeros((POOL_HALO, MIX), F32)

    def project(k):
        r0 = POOL_HALO + k * ts
        ext_ref[r0:r0 + ts, :] = _dot(h_ref[0, k * ts:(k + 1) * ts, :], win_ref[...])

    def work(k):
        r0 = POOL_HALO + k * ts
        t = (si * sub + k) * ts + lax.broadcasted_iota(jnp.int32, (ts, 1), 0)
        for g, w in enumerate(POOL_WINDOWS):
            lo = g * POOL_GROUP
            win = ext_ref[r0 - POOL_HALO:r0 + ts, lo:lo + POOL_GROUP]
            u_g = win[POOL_HALO:]
            shift = 1
            while shift < w:
                win = win + pltpu.roll(win, shift, 0)
                shift *= 2
            win = win[POOL_HALO:]
            count = jnp.minimum(t + 1, w).astype(F32)
            d = win / count - u_g
            y = _dot(d.astype(BF16), wp_ref[g])
            y_ref[k * ts:(k + 1) * ts, lo:lo + POOL_GROUP] = (y * ps_ref[:, lo:lo + POOL_GROUP]).astype(BF16)

    _run_ahead(sub, project, work)
    ext_ref[0:POOL_HALO, :] = ext_ref[sub * ts:sub * ts + POOL_HALO, :]
    o_ref[0] = y_ref[...]


def _pool(h, l, win, wp, ps):
    b, s, d = h.shape
    ts = min(SEQ_TILE, s)
    sub = _sub_tiles(s, ts, POOL_SUB)
    rows = sub * ts
    return pl.pallas_call(
        functools.partial(_pool_kernel, ts=ts, sub=sub),
        grid=(b, s // rows),
        in_specs=[pl.BlockSpec((1, rows, d), lambda i, j: (i, j, 0)), _layer(win, l), _layer(wp, l), _layer(ps, l)],
        out_specs=pl.BlockSpec((1, rows, MIX), lambda i, j: (i, j, 0)),
        out_shape=jax.ShapeDtypeStruct((b, s, MIX), BF16),
        scratch_shapes=[pltpu.VMEM((POOL_HALO + rows, MIX), F32), pltpu.VMEM((rows, MIX), BF16)],
        compiler_params=_params("arbitrary", "arbitrary"),
        name="pool_mixer",
    )(h, win, wp, ps)


def _causal_conv(buf_ref, r0, c0, width, cw_ref, cb_ref, ts):
    def rows(back):
        return buf_ref[r0 - back:r0 - back + ts, c0:c0 + width]

    acc = rows(0) * cw_ref[CONV_WIDTH - 1:CONV_WIDTH, :] + cb_ref[...]
    for jj in range(CONV_WIDTH - 1):
        acc = acc + rows(CONV_WIDTH - 1 - jj) * cw_ref[jj:jj + 1, :]
    return acc


SSD_IN_COLS = MIX + SSD_XBC + LANES
SSD_GROUP_COLS = MIX // SSD_GROUPS
SSD_GROUP_HEADS = SSD_HEADS // SSD_GROUPS


def _ssd_kernel(h_ref, win_ref, cw_ref, cb_ref, dtb_ref, alog_ref, dskip_ref, ng_ref,
                tri_ref, expand_ref, bd_ref, o_ref, u_ref, xbc_ref, y_ref, state_ref, *, ts, sub):
    si = pl.program_id(1)
    lc = SSD_CHUNK
    row = lax.broadcasted_iota(jnp.int32, (lc, lc), 0)
    col = lax.broadcasted_iota(jnp.int32, (lc, lc), 1)
    tril = row >= col
    lane = lax.broadcasted_iota(jnp.int32, (1, LANES), 1)

    @pl.when(si == 0)
    def _():
        state_ref[...] = jnp.zeros(state_ref.shape, F32)
        u_ref[0:CONV_HALO, :] = jnp.zeros((CONV_HALO, SSD_IN_COLS), F32)

    def project(k):
        r0 = CONV_HALO + k * ts
        u_ref[r0:r0 + ts, :] = _dot(h_ref[0, k * ts:(k + 1) * ts, :], win_ref[...])

    def work(k):
        r0 = CONV_HALO + k * ts
        xbc_ref[k] = _silu(_causal_conv(u_ref, r0, MIX, SSD_XBC, cw_ref, cb_ref, ts))
        dt_all = _softplus(u_ref[r0:r0 + ts, MIX + SSD_XBC:] + dtb_ref[...])
        a_all = dt_all * (-jnp.exp(alog_ref[...]))
        ys = []
        for c in range(ts // lc):
            c0 = c * lc
            xs = xbc_ref[k, c0:c0 + lc, 0:MIX]
            bm = xbc_ref[k, c0:c0 + lc, MIX:MIX + LANES]
            cm = xbc_ref[k, c0:c0 + lc, MIX + LANES:MIX + 2 * LANES]
            dt = dt_all[c0:c0 + lc]
            cs = _sel_dot_left(tri_ref[...], a_all[c0:c0 + lc])
            cs_t = cs.T
            dt_t = dt.T
            cs_end = cs[lc - 1:lc, :]
            per_head = jnp.concatenate([jnp.exp(cs), dt * jnp.exp(cs_end - cs)], axis=0)
            wide = _sel_dot2(per_head, expand_ref[...])
            ecs_x, dd_x = wide[0:lc], wide[lc:2 * lc]
            xs_b = xs.astype(BF16)
            xdd = (xs * dd_x).astype(BF16)
            y_parts = []
            for grp in range(SSD_GROUPS):
                in_grp = (lane >= grp * SSD_STATE) & (lane < (grp + 1) * SSD_STATE)
                cm_g = jnp.where(in_grp, cm, 0.0).astype(BF16)
                bm_g = jnp.where(in_grp, bm, 0.0).astype(BF16)
                cb = _dot_nt(cm_g, bm_g)
                gc = grp * SSD_GROUP_COLS
                masks = []
                for hl in range(SSD_GROUP_HEADS):
                    hd = grp * SSD_GROUP_HEADS + hl
                    seg = jnp.where(tril, cs[:, hd:hd + 1] - cs_t[hd:hd + 1, :], NEG_BIG)
                    masks.append((cb * jnp.exp(seg) * dt_t[hd:hd + 1, :]).astype(BF16))
                m_cat = jnp.concatenate(masks, axis=1)
                x_bd = jnp.tile(xs_b[:, gc:gc + SSD_GROUP_COLS], (SSD_GROUP_HEADS, 1)) * bd_ref[...]
                y_diag = _dot(m_cat, x_bd)
                st = state_ref[grp]
                y_off = _dot(cm_g, st.astype(BF16)) * ecs_x[:, gc:gc + SSD_GROUP_COLS]
                state_ref[grp] = (st * ecs_x[lc - 1:lc, gc:gc + SSD_GROUP_COLS]
                                  + _dot_tn(bm_g, xdd[:, gc:gc + SSD_GROUP_COLS]))
                y_parts.append(y_diag + y_off)
            ys.append(jnp.concatenate(y_parts, axis=1) + xs * dskip_ref[...])
        y = jnp.concatenate(ys, axis=0) * _silu(u_ref[r0:r0 + ts, 0:MIX])
        y_ref[k * ts:(k + 1) * ts, :] = _rmsnorm(y, ng_ref[...]).astype(BF16)

    _run_ahead(sub, project, work)
    u_ref[0:CONV_HALO, :] = u_ref[sub * ts:sub * ts + CONV_HALO, :]
    o_ref[0] = y_ref[...]


def _sel_dot_left(sel, a):
    hi, mid, lo = _split3(a)
    return _dot(sel, hi) + _dot(sel, mid) + _dot(sel, lo)


def _ssd_constants():
    lc = SSD_CHUNK
    tri = (jnp.arange(lc)[:, None] >= jnp.arange(lc)[None, :]).astype(BF16)
    expand = (jnp.arange(LANES)[:, None] == (jnp.arange(MIX)[None, :] // SSD_HEADDIM)).astype(BF16)
    rows = jnp.arange(SSD_GROUP_HEADS * lc)[:, None] // lc
    cols = jnp.arange(SSD_GROUP_COLS)[None, :] // SSD_HEADDIM
    bd = (rows == cols).astype(BF16)
    return tri, expand, bd


def _ssd(h, l, win, cw, cb, dtb, alog, dskip, ng):
    b, s, d = h.shape
    ts = min(SEQ_TILE, s)
    sub = _sub_tiles(s, ts, SSD_SUB)
    rows = sub * ts
    tri, expand, bd = _ssd_constants()
    return pl.pallas_call(
        functools.partial(_ssd_kernel, ts=ts, sub=sub),
        grid=(b, s // rows),
        in_specs=[pl.BlockSpec((1, rows, d), lambda i, j: (i, j, 0)), _layer(win, l), _layer(cw, l), _layer(cb, l),
                  _layer(dtb, l), _layer(alog, l), _layer(dskip, l), _layer(ng, l),
                  _full(tri.shape), _full(expand.shape), _full(bd.shape)],
        out_specs=pl.BlockSpec((1, rows, MIX), lambda i, j: (i, j, 0)),
        out_shape=jax.ShapeDtypeStruct((b, s, MIX), BF16),
        scratch_shapes=[pltpu.VMEM((CONV_HALO + rows, SSD_IN_COLS), F32),
                        pltpu.VMEM((sub, ts, SSD_XBC), F32),
                        pltpu.VMEM((rows, MIX), BF16),
                        pltpu.VMEM((SSD_GROUPS, LANES, SSD_GROUP_COLS), F32)],
        compiler_params=_params("arbitrary", "arbitrary"),
        name="ssd_mixer",
    )(h, win, cw, cb, dtb, alog, dskip, ng, tri, expand, bd)


def _lru_kernel(hn_ref, win_ref, cw_ref, cb_ref, wai_ref, bai_ref, lam_ref, o_ref,
                u_ref, a_ref, h_ref, y_ref, carry_ref, *, ts, sub):
    si = pl.program_id(1)
    rowi = lax.broadcasted_iota(jnp.int32, (SUBLANES, MIX), 0)

    @pl.when(si == 0)
    def _():
        carry_ref[...] = jnp.zeros(carry_ref.shape, F32)
        u_ref[0:CONV_HALO, :] = jnp.zeros((CONV_HALO, 2 * MIX), F32)

    def project(k):
        r0 = CONV_HALO + k * ts
        u_ref[r0:r0 + ts, :] = _dot(hn_ref[0, k * ts:(k + 1) * ts, :], win_ref[...])

    def work(k):
        r0 = CONV_HALO + k * ts
        xc = _causal_conv(u_ref, r0, MIX, MIX, cw_ref, cb_ref, ts)
        ri = _dot(xc.astype(BF16), wai_ref[...]) + bai_ref[...]
        r_t = _sigmoid(ri[:, :MIX])
        i_t = _sigmoid(ri[:, MIX:])
        log_a = (-LRU_C) * r_t * _softplus(-lam_ref[...])
        a_t = jnp.exp(log_a)
        mult = jnp.sqrt(jnp.tanh(-log_a) * (a_t * a_t + 1.0))
        a_ref[k] = a_t
        h_ref[k] = xc * i_t * mult
        carry = carry_ref[...]
        for blk in range(ts // SUBLANES):
            b0 = blk * SUBLANES
            a8 = a_ref[k, b0:b0 + SUBLANES, :]
            u8 = h_ref[k, b0:b0 + SUBLANES, :]
            for kk in (1, 2, 4):
                keep = rowi >= kk
                u8 = jnp.where(keep, a8 * pltpu.roll(u8, kk, 0) + u8, u8)
                a8 = jnp.where(keep, a8 * pltpu.roll(a8, kk, 0), a8)
            h8 = u8 + a8 * carry
            h_ref[k, b0:b0 + SUBLANES, :] = h8
            carry = h8[SUBLANES - 1:SUBLANES, :]
        carry_ref[...] = carry
        y_ref[k * ts:(k + 1) * ts, :] = (h_ref[k] * _gelu_tanh(u_ref[r0:r0 + ts, 0:MIX])).astype(BF16)

    _run_ahead(sub, project, work)
    u_ref[0:CONV_HALO, :] = u_ref[sub * ts:sub * ts + CONV_HALO, :]
    o_ref[0] = y_ref[...]


def _block_diag(w):
    depth, nb, n, _ = w.shape
    eye = jnp.eye(nb, dtype=w.dtype)
    return (eye[:, None, :, None] * w[:, :, :, None, :]).reshape(depth, nb * n, nb * n)


def _lru(h, l, win, cw, cb, wai, bai, lam):
    b, s, d = h.shape
    ts = min(SEQ_TILE, s)
    sub = _sub_tiles(s, ts, LRU_SUB)
    rows = sub * ts
    return pl.pallas_call(
        functools.partial(_lru_kernel, ts=ts, sub=sub),
        grid=(b, s // rows),
        in_specs=[pl.BlockSpec((1, rows, d), lambda i, j: (i, j, 0)), _layer(win, l), _layer(cw, l), _layer(cb, l),
                  _layer(wai, l), _layer(bai, l), _layer(lam, l)],
        out_specs=pl.BlockSpec((1, rows, MIX), lambda i, j: (i, j, 0)),
        out_shape=jax.ShapeDtypeStruct((b, s, MIX), BF16),
        scratch_shapes=[pltpu.VMEM((CONV_HALO + rows, 2 * MIX), F32), pltpu.VMEM((sub, ts, MIX), F32),
                        pltpu.VMEM((sub, ts, MIX), F32), pltpu.VMEM((rows, MIX), BF16), pltpu.VMEM((1, MIX), F32)],
        compiler_params=_params("arbitrary", "arbitrary"),
        name="rglru_mixer",
    )(h, win, cw, cb, wai, bai, lam)


def _merge_kernel(x_ref, h_ref, wg_ref, ya_ref, yb_ref, yc_ref, yd_ref, wb_ref, wo_ref, o_ref):
    h = h_ref[...]
    merged = None
    for n, y_ref in enumerate((ya_ref, yb_ref, yc_ref, yd_ref)):
        gate = _sigmoid(_dot(h, wg_ref[:, n * D_MODEL:(n + 1) * D_MODEL]))
        term = gate * _dot(y_ref[...], wb_ref[n])
        merged = term if merged is None else merged + term
    o_ref[...] = x_ref[...] + _dot(merged.astype(BF16), wo_ref[...])


def _merge(x2, h2, l, wg, ys, wb, wo):
    t, d = x2.shape
    tm = min(TOK_TILE, t)
    row = lambda w: pl.BlockSpec((tm, w), lambda i: (i, 0))
    return pl.pallas_call(
        _merge_kernel,
        grid=(t // tm,),
        in_specs=[row(d), row(d), _layer(wg, l), row(MIX), row(MIX), row(MIX), row(MIX),
                  _layer(wb, l), _layer(wo, l)],
        out_specs=row(d),
        out_shape=jax.ShapeDtypeStruct((t, d), F32),
        compiler_params=_params("arbitrary"),
        name="gated_merge",
    )(x2, h2, wg, *ys, wb, wo)


FF_CHUNK = 1024


def _mlp_ple_kernel(x_ref, g_ref, w1_ref, w2_ref, gp_ref, wpg_ref, p_ref, wple_ref, gn_ref, *out_refs, last):
    x = x_ref[...]
    h = _rmsnorm(x, g_ref[...]).astype(BF16)
    acc = x
    for c in range(D_FF // FF_CHUNK):
        hid = jnp.maximum(_dot(h, w1_ref[:, c * FF_CHUNK:(c + 1) * FF_CHUNK]), 0.0)
        acc = acc + _dot((hid * hid).astype(BF16), w2_ref[c * FF_CHUNK:(c + 1) * FF_CHUNK, :])
    gate = _sigmoid(_dot(_rmsnorm(acc, gp_ref[...]).astype(BF16), wpg_ref[...]))
    y = acc + _dot(p_ref[0].astype(BF16), wple_ref[...]) * gate
    normed = _rmsnorm(y, gn_ref[...])
    if last:
        out_refs[0][...] = normed
    else:
        out_refs[0][...] = y
        out_refs[1][...] = normed.astype(BF16)


def _mlp_ple(x2, l, g, w1, w2, gp, wpg, p3, wple, g_next, last):
    t, d = x2.shape
    tm = min(TOK_TILE, t)
    row = lambda w: pl.BlockSpec((tm, w), lambda i: (i, 0))
    out_specs = [row(d)] if last else [row(d), row(d)]
    out_shape = [jax.ShapeDtypeStruct((t, d), F32)] + ([] if last else [jax.ShapeDtypeStruct((t, d), BF16)])
    return pl.pallas_call(
        functools.partial(_mlp_ple_kernel, last=last),
        grid=(t // tm,),
        in_specs=[row(d), _layer(g, l), _layer(w1, l, True), _layer(w2, l, True), _layer(gp, l),
                  _layer(wpg, l, True), pl.BlockSpec((1, tm, PLE_DIM), lambda i: (l, i, 0)),
                  _layer(wple, l, True), _layer(g_next, 0)],
        out_specs=out_specs,
        out_shape=out_shape,
        compiler_params=_params("arbitrary"),
        name="mlp_ple",
    )(x2, g, w1, w2, gp, wpg, p3, wple, g_next)


_SPLIT = (Q_LORA, KV_LORA, QK_ROPE, MIX, MIX, SSD_XBC, SSD_HEADS, MIX, MIX, N_BRANCH * D_MODEL)


def _col_offsets():
    offs, acc = [], 0
    for sz in _SPLIT:
        offs.append(acc)
        acc += sz
    return offs


def kernel(x, p, positions, g_mix, w_in, q_norm, w_uq, kv_norm, w_ukv, w_pool, pool_scale,
           ssd_conv_w, ssd_conv_b, ssd_dt_bias, ssd_a_log, ssd_d, ssd_norm,
           lru_conv_w, lru_conv_b, lru_w_a, lru_b_a, lru_w_i, lru_b_i, lru_lambda,
           w_branch, w_out, g_mlp, w_ff1, w_ff2, g_ple, w_ple_gate, w_ple, g_final):
    b, s, d = x.shape
    depth = w_in.shape[0]
    t = b * s
    offs = _col_offsets()
    o_pool, o_z, o_xbc, o_dt, o_lg, o_lx, o_gate = offs[3], offs[4], offs[5], offs[6], offs[7], offs[8], offs[9]
    cosq, sinq = _rope_tables(positions)
    bf = lambda a: a.astype(BF16)
    vec = lambda a: a.reshape(depth, 1, -1)
    win_a, wq, wk, wv = _mla_weights(w_in, w_uq, w_ukv)
    win_b = bf(w_in[..., o_pool:o_pool + MIX])
    w_dt = jnp.pad(w_in[..., o_dt:o_dt + SSD_HEADS], ((0, 0), (0, 0), (0, LANES - SSD_HEADS)))
    win_c = bf(jnp.concatenate([w_in[..., o_z:o_z + MIX], w_in[..., o_xbc:o_xbc + SSD_XBC], w_dt], axis=-1))
    head_lanes = lambda a: vec(jnp.pad(a, ((0, 0), (0, LANES - SSD_HEADS))))
    win_d = bf(w_in[..., o_lg:o_lg + 2 * MIX])
    wai = bf(jnp.concatenate([_block_diag(lru_w_a), _block_diag(lru_w_i)], axis=-1))
    bai = vec(jnp.concatenate([lru_b_a, lru_b_i], axis=-1))
    wg, wb, wo = bf(w_in[..., o_gate:]), bf(w_branch), bf(w_out)
    w1, w2, wpg, wple = bf(w_ff1), bf(w_ff2), bf(w_ple_gate), bf(w_ple)
    gm, gmlp, gple = vec(g_mix), vec(g_mlp), vec(g_ple)
    qn, kvn, ps = vec(q_norm), vec(kv_norm), vec(pool_scale)
    scb, dtb, alog = vec(ssd_conv_b), head_lanes(ssd_dt_bias), head_lanes(ssd_a_log)
    dskip, sng = vec(jnp.repeat(ssd_d, SSD_HEADDIM, axis=-1)), vec(ssd_norm)
    lcb, lam = vec(lru_conv_b), vec(lru_lambda)
    wp = bf(w_pool)
    p3 = p.reshape(depth, t, PLE_DIM)
    x2 = x.reshape(t, d)
    h2 = None
    for l in range(depth):
        if h2 is None:
            q, k, v, h = _mla_proj(x, gm, l, win_a, qn, wq, kvn, wk, wv, cosq, sinq)
            h2 = h.reshape(t, d)
        else:
            h = h2.reshape(b, s, d)
            q, k, v = _mla_proj(h, None, l, win_a, qn, wq, kvn, wk, wv, cosq, sinq)
        y_a = _attention(q, k, v)
        y_b = _pool(h, l, win_b, wp, ps)
        y_c = _ssd(h, l, win_c, ssd_conv_w, scb, dtb, alog, dskip, sng)
        y_d = _lru(h, l, win_d, lru_conv_w, lcb, wai, bai, lam)
        ys = [y.reshape(t, MIX) for y in (y_a, y_b, y_c, y_d)]
        x2 = _merge(x2, h2, l, wg, ys, wb, wo)
        last = l == depth - 1
        g_next = g_final.reshape(1, 1, d) if last else gm[l + 1:l + 2]
        outs = _mlp_ple(x2, l, gmlp, w1, w2, gple, wpg, p3, wple, g_next, last)
        if last:
            x2 = outs[0]
        else:
            x2, h2 = outs
    return x2.reshape(b, s, d)
```

```python
import functools

import jax
import jax.numpy as jnp
import numpy as np
from jax import lax
from jax.experimental import pallas as pl
from jax.experimental.pallas import tpu as pltpu

F32 = jnp.float32
BF16 = jnp.bfloat16

D_MODEL = 1024
MIX = 512
N_BRANCH = 4
HEADS = 8
QK_NOPE = 64
QK_ROPE = 32
V_HEAD = 64
Q_LORA = 384
KV_LORA = 256
ROPE_THETA = 10000.0
POOL_WINDOWS = (2, 4, 8, 16)
POOL_GROUP = 128
POOL_HALO = 16
SSD_HEADS = 8
SSD_HEADDIM = 64
SSD_GROUPS = 2
SSD_STATE = 64
SSD_CHUNK = 128
CONV_WIDTH = 4
CONV_HALO = 8
SSD_XBC = 768
LRU_BLOCKS = 8
LRU_BLOCK = 64
LRU_C = 8.0
D_FF = 4096
PLE_DIM = 256
EPS = 1e-6
LANES = 128
SUBLANES = 8
HEAD_PAD = 128
NEG_BIG = -1e30
LOG2_E = 1.4426950408889634
VMEM_LIMIT = 56 * 1024 * 1024

SEQ_TILE = 512
POOL_SUB = 4
SSD_SUB = 2
LRU_SUB = 4
TOK_TILE = 1024
ATT_TILE = 512
ATT_HEADS = 4
ATT_SKEW = 1


def _dot(a, b):
    return jnp.dot(a, b, preferred_element_type=F32)


def _dot_nt(a, b):
    return lax.dot_general(a, b, (((1,), (1,)), ((), ())), preferred_element_type=F32)


def _dot_tn(a, b):
    return lax.dot_general(a, b, (((0,), (0,)), ((), ())), preferred_element_type=F32)


def _split3(a):
    hi = a.astype(BF16)
    r1 = a - hi.astype(F32)
    mid = r1.astype(BF16)
    lo = (r1 - mid.astype(F32)).astype(BF16)
    return hi, mid, lo


def _sel_dot2(a, sel):
    hi = a.astype(BF16)
    mid = (a - hi.astype(F32)).astype(BF16)
    return _dot(hi, sel) + _dot(mid, sel)


def _rmsnorm(x, g):
    return x * lax.rsqrt(jnp.mean(x * x, axis=-1, keepdims=True) + EPS) * g


def _sigmoid(x):
    return jax.nn.sigmoid(x)


def _silu(x):
    return x * jax.nn.sigmoid(x)


def _softplus(x):
    return jnp.maximum(x, 0.0) + jnp.log1p(jnp.exp(-jnp.abs(x)))


def _gelu_tanh(x):
    return 0.5 * x * (1.0 + jnp.tanh(0.7978845608028654 * (x + 0.044715 * (x * x * x))))


def _params(*sem):
    return pltpu.CompilerParams(dimension_semantics=sem, vmem_limit_bytes=VMEM_LIMIT)


def _full(shape):
    n = len(shape)
    return pl.BlockSpec(shape, lambda *_: (0,) * n)


def _layer(stack, l, resident=False):
    n = stack.ndim - 1
    mode = dict(pipeline_mode=pl.Buffered(1)) if resident else {}
    return pl.BlockSpec((None,) + tuple(stack.shape[1:]), lambda *_: (l,) + (0,) * n, **mode)


ROPE_HALF = QK_ROPE // 2
ROPE_PACK = LANES // ROPE_HALF


def _rope_kernel(ang_ref, place_ref, cos_ref, sin_ref, *, blk):
    a = ang_ref[...]
    cos_parts = _split3(jnp.cos(a))
    sin_parts = _split3(jnp.sin(a))
    lane = lax.broadcasted_iota(jnp.int32, (1, LANES), 1)
    ones = (lane < QK_NOPE).astype(F32)
    for i in range(ROPE_PACK):
        ci = sum(_dot(part, place_ref[0, i]) for part in cos_parts) + ones
        si = sum(_dot(part, place_ref[1, i]) for part in sin_parts)
        cos_ref[pl.ds(i, blk, stride=ROPE_PACK), :] = ci
        sin_ref[pl.ds(i, blk, stride=ROPE_PACK), :] = si


def _rope_placement():
    place = np.zeros((2, ROPE_PACK, LANES, LANES), np.float32)
    for i in range(ROPE_PACK):
        for j in range(ROPE_HALF):
            src = i * ROPE_HALF + j
            place[0, i, src, QK_NOPE + j] = 1.0
            place[0, i, src, QK_NOPE + ROPE_HALF + j] = 1.0
            place[1, i, src, QK_NOPE + j] = -1.0
            place[1, i, src, QK_NOPE + ROPE_HALF + j] = 1.0
    return jnp.asarray(place, BF16)


def _rope_tables(positions):
    b, s = positions.shape
    t = b * s
    inv = 1.0 / (ROPE_THETA ** (jnp.arange(0, QK_ROPE, 2, dtype=F32) / QK_ROPE))
    rows = t // ROPE_PACK
    pos = positions.astype(F32).reshape(rows, ROPE_PACK)
    ang = jnp.repeat(pos, ROPE_HALF, axis=1) * jnp.tile(inv, ROPE_PACK)[None, :]
    blk = min(rows, 512)
    place = _rope_placement()
    out_spec = pl.BlockSpec((blk * ROPE_PACK, LANES), lambda i: (i, 0))
    cosq, sinq = pl.pallas_call(
        functools.partial(_rope_kernel, blk=blk),
        grid=(rows // blk,),
        in_specs=[pl.BlockSpec((blk, LANES), lambda i: (i, 0)), _full(place.shape)],
        out_specs=[out_spec, out_spec],
        out_shape=[jax.ShapeDtypeStruct((t, LANES), F32)] * 2,
        compiler_params=_params("arbitrary"),
        name="rope_tables",
    )(ang, place)
    return cosq.reshape(b, s, LANES), sinq.reshape(b, s, LANES)


MLA_IN_COLS = Q_LORA + KV_LORA + 2 * HEAD_PAD


def _mla_proj_kernel(*refs, norm_input):
    if norm_input:
        x_ref, g_ref, *refs = refs
    else:
        h_ref, *refs = refs
    win_ref, qn_ref, wq_ref, kvn_ref, wk_ref, wv_ref, cos_ref, sin_ref, q_out, k_out, v_out, *h_out = refs
    if norm_input:
        h = _rmsnorm(x_ref[0], g_ref[...]).astype(BF16)
        h_out[0][0] = h
    else:
        h = h_ref[0]
    scale = (QK_NOPE + QK_ROPE) ** -0.5 * LOG2_E
    u = _dot(h, win_ref[...])
    c_q = u[:, :Q_LORA]
    c_kv = u[:, Q_LORA:Q_LORA + KV_LORA]
    kr = u[:, Q_LORA + KV_LORA:Q_LORA + KV_LORA + HEAD_PAD]
    kr_sw = u[:, Q_LORA + KV_LORA + HEAD_PAD:]
    cosq = cos_ref[0]
    sinq = sin_ref[0]
    cqn = _rmsnorm(c_q, qn_ref[...]).astype(BF16)
    qq = _dot(cqn, wq_ref[...])
    ckvn = _rmsnorm(c_kv, kvn_ref[...]).astype(BF16)
    kn = _dot(ckvn, wk_ref[...])
    v = _dot(ckvn, wv_ref[...])
    lane = lax.broadcasted_iota(jnp.int32, (1, HEAD_PAD), 1)
    ones_lane = (lane == V_HEAD).astype(F32)
    k_rope = kr * cosq + kr_sw * sinq
    wide = HEADS * HEAD_PAD
    for hd in range(HEADS):
        lo = hd * HEAD_PAD
        q_h = (qq[:, lo:lo + HEAD_PAD] * cosq + qq[:, wide + lo:wide + lo + HEAD_PAD] * sinq) * scale
        q_out[0, hd] = q_h.astype(BF16)
        k_out[0, hd] = (kn[:, lo:lo + HEAD_PAD] + k_rope).astype(BF16)
        v_out[0, hd] = (v[:, lo:lo + HEAD_PAD] + ones_lane).astype(BF16)


def _mla_weights(w_in, w_uq, w_ukv):
    depth = w_in.shape[0]
    half = QK_ROPE // 2
    pad_l, pad_r = QK_NOPE, HEAD_PAD - QK_NOPE - QK_ROPE
    o = Q_LORA + KV_LORA
    w_kr = w_in[..., o:o + QK_ROPE]
    w_kr_sw = jnp.concatenate([w_kr[..., half:], w_kr[..., :half]], axis=-1)
    place = lambda w: jnp.pad(w, ((0, 0), (0, 0), (pad_l, pad_r)))
    win = jnp.concatenate([w_in[..., :o], place(w_kr), place(w_kr_sw)], axis=-1).astype(BF16)
    wq4 = w_uq.reshape(depth, Q_LORA, HEADS, QK_NOPE + QK_ROPE)
    wq_a = jnp.pad(wq4, ((0, 0), (0, 0), (0, 0), (0, pad_r))).reshape(depth, Q_LORA, HEADS * HEAD_PAD)
    rope = wq4[..., QK_NOPE:]
    rope_sw = jnp.concatenate([rope[..., half:], rope[..., :half]], axis=-1)
    wq_b = jnp.pad(rope_sw, ((0, 0), (0, 0), (0, 0), (pad_l, pad_r))).reshape(depth, Q_LORA, HEADS * HEAD_PAD)
    wq = jnp.concatenate([wq_a, wq_b], axis=-1).astype(BF16)
    wkv4 = w_ukv.reshape(depth, KV_LORA, HEADS, QK_NOPE + V_HEAD)
    pad_head = lambda w, n: jnp.pad(w, ((0, 0), (0, 0), (0, 0), (0, HEAD_PAD - n))).reshape(
        depth, KV_LORA, HEADS * HEAD_PAD).astype(BF16)
    return win, wq, pad_head(wkv4[..., :QK_NOPE], QK_NOPE), pad_head(wkv4[..., QK_NOPE:], V_HEAD)


def _mla_proj(stream, gain, l, win, qn, wq, kvn, wk, wv, cosq, sinq):
    b, s, d = stream.shape
    ts = min(TOK_TILE, s)
    tile = lambda w: pl.BlockSpec((1, ts, w), lambda i, j: (i, j, 0))
    head_out = lambda n: pl.BlockSpec((1, n, ts, HEAD_PAD), lambda i, j: (i, 0, j, 0))
    norm_input = gain is not None
    lead = [stream, gain] if norm_input else [stream]
    lead_specs = [tile(d), _layer(gain, l)] if norm_input else [tile(d)]
    return pl.pallas_call(
        functools.partial(_mla_proj_kernel, norm_input=norm_input),
        grid=(b, s // ts),
        in_specs=lead_specs + [_layer(win, l), _layer(qn, l), _layer(wq, l), _layer(kvn, l), _layer(wk, l),
                               _layer(wv, l), tile(HEAD_PAD), tile(HEAD_PAD)],
        out_specs=[head_out(HEADS)] * 3 + ([tile(d)] if norm_input else []),
        out_shape=[jax.ShapeDtypeStruct((b, HEADS, s, HEAD_PAD), BF16)] * 3
        + ([jax.ShapeDtypeStruct((b, s, d), BF16)] if norm_input else []),
        compiler_params=_params("arbitrary", "arbitrary"),
        name="mla_proj",
    )(*lead, win, qn, wq, kvn, wk, wv, cosq, sinq)


def _attn_kernel(q_ref, k_ref, v_ref, o_ref, sa_ref, sb_ref, m_ref, acc_ref, *, tq, nq):
    half = tq // 2
    lane = lax.broadcasted_iota(jnp.int32, (tq, LANES), 1)
    diag_parts = []
    for r0, keys in ((0, half), (half, tq)):
        row = lax.broadcasted_iota(jnp.int32, (half, keys), 0) + r0
        col = lax.broadcasted_iota(jnp.int32, (half, keys), 1)
        diag_parts.append((r0, keys, row >= col))

    def produce(dst_ref, hh, qi, kb, diag):
        q0 = pl.multiple_of(qi * tq, tq)
        k0 = pl.multiple_of(kb * tq, tq)
        parts = [(r0, keys) for r0, keys, _ in diag_parts] if diag else [(0, tq)]
        for r0, keys in parts:
            rows = tq if not diag else half
            dst_ref[hh, r0:r0 + rows, 0:keys] = _dot_nt(q_ref[0, hh, pl.ds(q0 + r0, rows), :],
                                                        k_ref[0, hh, pl.ds(k0, keys), :])

    def softmax_pv(src_ref, hh, k0, r0, rows, keys, mask):
        sc = src_ref[hh, r0:r0 + rows, 0:keys]
        if mask is not None:
            sc = jnp.where(mask, sc, NEG_BIG)
        m = m_ref[hh, r0:r0 + rows, :]
        m_new = jnp.maximum(m, jnp.max(sc, axis=-1, keepdims=True))
        p = jnp.exp2(sc - jnp.tile(m_new, (1, keys // LANES))).astype(BF16)
        acc_ref[hh, r0:r0 + rows, :] = (jnp.exp2(m - m_new) * acc_ref[hh, r0:r0 + rows, :]
                                        + _dot(p, v_ref[0, hh, pl.ds(k0, keys), :]))
        m_ref[hh, r0:r0 + rows, :] = m_new

    def consume(src_ref, hh, kb, diag):
        k0 = pl.multiple_of(kb * tq, tq)
        if diag:
            for r0, keys, mask in diag_parts:
                softmax_pv(src_ref, hh, k0, r0, half, keys, mask)
        else:
            softmax_pv(src_ref, hh, k0, 0, tq, tq, None)

    def step(src_ref, dst_ref, kb, diag, nqi, nkb, next_diag):
        for hh in range(min(ATT_SKEW, ATT_HEADS)):
            produce(dst_ref, hh, nqi, nkb, next_diag)
        for hh in range(ATT_HEADS):
            consume(src_ref, hh, kb, diag)
            if hh + ATT_SKEW < ATT_HEADS:
                produce(dst_ref, hh + ATT_SKEW, nqi, nkb, next_diag)

    def reset():
        m_ref[...] = jnp.full(m_ref.shape, NEG_BIG, F32)
        acc_ref[...] = jnp.zeros(acc_ref.shape, F32)

    def finalize(qi):
        q0 = pl.multiple_of(qi * tq, tq)
        for pair in range(ATT_HEADS // 2):
            acc0 = acc_ref[2 * pair]
            acc1 = acc_ref[2 * pair + 1]
            o0 = acc0 / acc0[:, V_HEAD:V_HEAD + 1]
            o1 = acc1 / acc1[:, V_HEAD:V_HEAD + 1]
            o_ref[0, pl.ds(q0, tq), pair * LANES:(pair + 1) * LANES] = jnp.where(
                lane < V_HEAD, o0, pltpu.roll(o1, V_HEAD, 1)).astype(BF16)

    n_steps = nq * (nq + 1) // 2
    bufs = (sa_ref, sb_ref)
    reset()
    for hh in range(ATT_HEADS):
        produce(sa_ref, hh, 0, 0, True)

    def body(t, carry):
        qi, kb = carry
        diag = kb == qi
        next_diag = kb + 1 == qi
        nqi = jnp.where(diag, qi + 1, qi)
        nkb = jnp.where(diag, 0, kb + 1)
        for par in range(2):
            src, dst = bufs[par], bufs[1 - par]
            here = t % 2 == par

            @pl.when(here & jnp.logical_not(diag) & jnp.logical_not(next_diag))
            def _():
                step(src, dst, kb, False, nqi, nkb, False)

            @pl.when(here & next_diag)
            def _():
                step(src, dst, kb, False, nqi, nkb, True)

            @pl.when(here & diag)
            def _():
                step(src, dst, kb, True, nqi, nkb, False)
                finalize(qi)
                reset()

        return nqi, nkb

    lax.fori_loop(0, n_steps - 1, body, (jnp.int32(0), jnp.int32(0)))
    for hh in range(ATT_HEADS):
        consume(bufs[(n_steps - 1) % 2], hh, nq - 1, True)
    finalize(nq - 1)


def _attention(q, k, v):
    b, _, s, _ = q.shape
    tq = min(ATT_TILE, s)
    spec = pl.BlockSpec((1, ATT_HEADS, s, HEAD_PAD), lambda i, hg: (i, hg, 0, 0))
    return pl.pallas_call(
        functools.partial(_attn_kernel, tq=tq, nq=s // tq),
        grid=(b, HEADS // ATT_HEADS),
        in_specs=[spec, spec, spec],
        out_specs=pl.BlockSpec((1, s, ATT_HEADS * V_HEAD), lambda i, hg: (i, 0, hg)),
        out_shape=jax.ShapeDtypeStruct((b, s, MIX), BF16),
        scratch_shapes=[pltpu.VMEM((ATT_HEADS, tq, tq), F32), pltpu.VMEM((ATT_HEADS, tq, tq), F32),
                        pltpu.VMEM((ATT_HEADS, tq, LANES), F32), pltpu.VMEM((ATT_HEADS, tq, HEAD_PAD), F32)],
        compiler_params=_params("arbitrary", "arbitrary"),
        name="mla_attention",
    )(q, k, v)


def _sub_tiles(seq_len, ts, want):
    n = seq_len // ts
    return max(k for k in range(1, want + 1) if n % k == 0)


def _run_ahead(sub, project, work):
    project(0)
    for k in range(sub):
        if k + 1 < sub:
            project(k + 1)
        work(k)


def _pool_kernel(h_ref, win_ref, wp_ref, ps_ref, o_ref, ext_ref, y_ref, *, ts, sub):
    si = pl.program_id(1)

    @pl.when(si == 0)
    def _():
        ext_ref[0:POOL_HALO, :] = jnp.zeros((POOL_HALO, MIX), F32)

    def project(k):
        r0 = POOL_HALO + k * ts
        ext_ref[r0:r0 + ts, :] = _dot(h_ref[0, k * ts:(k + 1) * ts, :], win_ref[...])

    def work(k):
        r0 = POOL_HALO + k * ts
        t = (si * sub + k) * ts + lax.broadcasted_iota(jnp.int32, (ts, 1), 0)
        for g, w in enumerate(POOL_WINDOWS):
            lo = g * POOL_GROUP
            win = ext_ref[r0 - POOL_HALO:r0 + ts, lo:lo + POOL_GROUP]
            u_g = win[POOL_HALO:]
            shift = 1
            while shift < w:
                win = win + pltpu.roll(win, shift, 0)
                shift *= 2
            win = win[POOL_HALO:]
            count = jnp.minimum(t + 1, w).astype(F32)
            d = win / count - u_g
            y = _dot(d.astype(BF16), wp_ref[g])
            y_ref[k * ts:(k + 1) * ts, lo:lo + POOL_GROUP] = (y * ps_ref[:, lo:lo + POOL_GROUP]).astype(BF16)

    _run_ahead(sub, project, work)
    ext_ref[0:POOL_HALO, :] = ext_ref[sub * ts:sub * ts + POOL_HALO, :]
    o_ref[0] = y_ref[...]


def _causal_conv(buf_ref, r0, c0, width, cw_ref, cb_ref, ts):
    def rows(back):
        return buf_ref[r0 - back:r0 - back + ts, c0:c0 + width]

    acc = rows(0) * cw_ref[CONV_WIDTH - 1:CONV_WIDTH, :] + cb_ref[...]
    for jj in range(CONV_WIDTH - 1):
        acc = acc + rows(CONV_WIDTH - 1 - jj) * cw_ref[jj:jj + 1, :]
    return acc


SSD_IN_COLS = MIX + SSD_XBC + LANES
SSD_GROUP_COLS = MIX // SSD_GROUPS
SSD_GROUP_HEADS = SSD_HEADS // SSD_GROUPS


def _ssd_kernel(h_ref, win_ref, cw_ref, cb_ref, dtb_ref, alog_ref, dskip_ref, ng_ref,
                tri_ref, expand_ref, bd_ref, o_ref, u_ref, xbc_ref, y_ref, state_ref, *, ts, sub):
    si = pl.program_id(1)
    lc = SSD_CHUNK
    row = lax.broadcasted_iota(jnp.int32, (lc, lc), 0)
    col = lax.broadcasted_iota(jnp.int32, (lc, lc), 1)
    tril = row >= col
    lane = lax.broadcasted_iota(jnp.int32, (1, LANES), 1)

    @pl.when(si == 0)
    def _():
        state_ref[...] = jnp.zeros(state_ref.shape, F32)
        u_ref[0:CONV_HALO, :] = jnp.zeros((CONV_HALO, SSD_IN_COLS), F32)

    def project(k):
        r0 = CONV_HALO + k * ts
        u_ref[r0:r0 + ts, :] = _dot(h_ref[0, k * ts:(k + 1) * ts, :], win_ref[...])

    def work(k):
        r0 = CONV_HALO + k * ts
        xbc_ref[k] = _silu(_causal_conv(u_ref, r0, MIX, SSD_XBC, cw_ref, cb_ref, ts))
        dt_all = _softplus(u_ref[r0:r0 + ts, MIX + SSD_XBC:] + dtb_ref[...])
        a_all = dt_all * (-jnp.exp(alog_ref[...]))
        ys = []
        for c in range(ts // lc):
            c0 = c * lc
            xs = xbc_ref[k, c0:c0 + lc, 0:MIX]
            bm = xbc_ref[k, c0:c0 + lc, MIX:MIX + LANES]
            cm = xbc_ref[k, c0:c0 + lc, MIX + LANES:MIX + 2 * LANES]
            dt = dt_all[c0:c0 + lc]
            cs = _sel_dot_left(tri_ref[...], a_all[c0:c0 + lc])
            cs_t = cs.T
            dt_t = dt.T
            cs_end = cs[lc - 1:lc, :]
            per_head = jnp.concatenate([jnp.exp(cs), dt * jnp.exp(cs_end - cs)], axis=0)
            wide = _sel_dot2(per_head, expand_ref[...])
            ecs_x, dd_x = wide[0:lc], wide[lc:2 * lc]
            xs_b = xs.astype(BF16)
            xdd = (xs * dd_x).astype(BF16)
            y_parts = []
            for grp in range(SSD_GROUPS):
                in_grp = (lane >= grp * SSD_STATE) & (lane < (grp + 1) * SSD_STATE)
                cm_g = jnp.where(in_grp, cm, 0.0).astype(BF16)
                bm_g = jnp.where(in_grp, bm, 0.0).astype(BF16)
                cb = _dot_nt(cm_g, bm_g)
                gc = grp * SSD_GROUP_COLS
                masks = []
                for hl in range(SSD_GROUP_HEADS):
                    hd = grp * SSD_GROUP_HEADS + hl
                    seg = jnp.where(tril, cs[:, hd:hd + 1] - cs_t[hd:hd + 1, :], NEG_BIG)
                    masks.append((cb * jnp.exp(seg) * dt_t[hd:hd + 1, :]).astype(BF16))
                m_cat = jnp.concatenate(masks, axis=1)
                x_bd = jnp.tile(xs_b[:, gc:gc + SSD_GROUP_COLS], (SSD_GROUP_HEADS, 1)) * bd_ref[...]
                y_diag = _dot(m_cat, x_bd)
                st = state_ref[grp]
                y_off = _dot(cm_g, st.astype(BF16)) * ecs_x[:, gc:gc + SSD_GROUP_COLS]
                state_ref[grp] = (st * ecs_x[lc - 1:lc, gc:gc + SSD_GROUP_COLS]
                                  + _dot_tn(bm_g, xdd[:, gc:gc + SSD_GROUP_COLS]))
                y_parts.append(y_diag + y_off)
            ys.append(jnp.concatenate(y_parts, axis=1) + xs * dskip_ref[...])
        y = jnp.concatenate(ys, axis=0) * _silu(u_ref[r0:r0 + ts, 0:MIX])
        y_ref[k * ts:(k + 1) * ts, :] = _rmsnorm(y, ng_ref[...]).astype(BF16)

    _run_ahead(sub, project, work)
    u_ref[0:CONV_HALO, :] = u_ref[sub * ts:sub * ts + CONV_HALO, :]
    o_ref[0] = y_ref[...]


def _sel_dot_left(sel, a):
    hi, mid, lo = _split3(a)
    return _dot(sel, hi) + _dot(sel, mid) + _dot(sel, lo)


def _ssd_constants():
    lc = SSD_CHUNK
    tri = (jnp.arange(lc)[:, None] >= jnp.arange(lc)[None, :]).astype(BF16)
    expand = (jnp.arange(LANES)[:, None] == (jnp.arange(MIX)[None, :] // SSD_HEADDIM)).astype(BF16)
    rows = jnp.arange(SSD_GROUP_HEADS * lc)[:, None] // lc
    cols = jnp.arange(SSD_GROUP_COLS)[None, :] // SSD_HEADDIM
    bd = (rows == cols).astype(BF16)
    return tri, expand, bd


def _ssd(h, l, win, cw, cb, dtb, alog, dskip, ng):
    b, s, d = h.shape
    ts = min(SEQ_TILE, s)
    sub = _sub_tiles(s, ts, SSD_SUB)
    rows = sub * ts
    tri, expand, bd = _ssd_constants()
    return pl.pallas_call(
        functools.partial(_ssd_kernel, ts=ts, sub=sub),
        grid=(b, s // rows),
        in_specs=[pl.BlockSpec((1, rows, d), lambda i, j: (i, j, 0)), _layer(win, l), _layer(cw, l), _layer(cb, l),
                  _layer(dtb, l), _layer(alog, l), _layer(dskip, l), _layer(ng, l),
                  _full(tri.shape), _full(expand.shape), _full(bd.shape)],
        out_specs=pl.BlockSpec((1, rows, MIX), lambda i, j: (i, j, 0)),
        out_shape=jax.ShapeDtypeStruct((b, s, MIX), BF16),
        scratch_shapes=[pltpu.VMEM((CONV_HALO + rows, SSD_IN_COLS), F32),
                        pltpu.VMEM((sub, ts, SSD_XBC), F32),
                        pltpu.VMEM((rows, MIX), BF16),
                        pltpu.VMEM((SSD_GROUPS, LANES, SSD_GROUP_COLS), F32)],
        compiler_params=_params("arbitrary", "arbitrary"),
        name="ssd_mixer",
    )(h, win, cw, cb, dtb, alog, dskip, ng, tri, expand, bd)


def _lru_kernel(hn_ref, win_ref, cw_ref, cb_ref, wai_ref, bai_ref, lam_ref, o_ref,
                u_ref, a_ref, h_ref, y_ref, carry_ref, *, ts, sub):
    si = pl.program_id(1)
    rowi = lax.broadcasted_iota(jnp.int32, (SUBLANES, MIX), 0)

    @pl.when(si == 0)
    def _():
        carry_ref[...] = jnp.zeros(carry_ref.shape, F32)
        u_ref[0:CONV_HALO, :] = jnp.zeros((CONV_HALO, 2 * MIX), F32)

    def project(k):
        r0 = CONV_HALO + k * ts
        u_ref[r0:r0 + ts, :] = _dot(hn_ref[0, k * ts:(k + 1) * ts, :], win_ref[...])

    def work(k):
        r0 = CONV_HALO + k * ts
        xc = _causal_conv(u_ref, r0, MIX, MIX, cw_ref, cb_ref, ts)
        ri = _dot(xc.astype(BF16), wai_ref[...]) + bai_ref[...]
        r_t = _sigmoid(ri[:, :MIX])
        i_t = _sigmoid(ri[:, MIX:])
        log_a = (-LRU_C) * r_t * _softplus(-lam_ref[...])
        a_t = jnp.exp(log_a)
        mult = jnp.sqrt(jnp.tanh(-log_a) * (a_t * a_t + 1.0))
        a_ref[k] = a_t
        h_ref[k] = xc * i_t * mult
        carry = carry_ref[...]
        for blk in range(ts // SUBLANES):
            b0 = blk * SUBLANES
            a8 = a_ref[k, b0:b0 + SUBLANES, :]
            u8 = h_ref[k, b0:b0 + SUBLANES, :]
            for kk in (1, 2, 4):
                keep = rowi >= kk
                u8 = jnp.where(keep, a8 * pltpu.roll(u8, kk, 0) + u8, u8)
                a8 = jnp.where(keep, a8 * pltpu.roll(a8, kk, 0), a8)
            h8 = u8 + a8 * carry
            h_ref[k, b0:b0 + SUBLANES, :] = h8
            carry = h8[SUBLANES - 1:SUBLANES, :]
        carry_ref[...] = carry
        y_ref[k * ts:(k + 1) * ts, :] = (h_ref[k] * _gelu_tanh(u_ref[r0:r0 + ts, 0:MIX])).astype(BF16)

    _run_ahead(sub, project, work)
    u_ref[0:CONV_HALO, :] = u_ref[sub * ts:sub * ts + CONV_HALO, :]
    o_ref[0] = y_ref[...]


def _block_diag(w):
    depth, nb, n, _ = w.shape
    eye = jnp.eye(nb, dtype=w.dtype)
    return (eye[:, None, :, None] * w[:, :, :, None, :]).reshape(depth, nb * n, nb * n)


N_POOL_IN, N_LRU_IN = 3, 6
N_POOL_SCRATCH = 2


def _pool_lru_kernel(h_ref, *refs, ts, sub):
    pool_in, refs = refs[:N_POOL_IN], refs[N_POOL_IN:]
    lru_in, refs = refs[:N_LRU_IN], refs[N_LRU_IN:]
    o_pool, o_lru, *scratch = refs
    _pool_kernel(h_ref, *pool_in, o_pool, *scratch[:N_POOL_SCRATCH], ts=ts, sub=sub)
    _lru_kernel(h_ref, *lru_in, o_lru, *scratch[N_POOL_SCRATCH:], ts=ts, sub=sub)


def _pool_lru(h, l, win_b, wp, ps, win_d, cw, cb, wai, bai, lam):
    b, s, d = h.shape
    ts = min(SEQ_TILE, s)
    sub = _sub_tiles(s, ts, min(POOL_SUB, LRU_SUB))
    rows = sub * ts
    block = lambda w: pl.BlockSpec((1, rows, w), lambda i, j: (i, j, 0))
    params = (win_b, wp, ps, win_d, cw, cb, wai, bai, lam)
    return pl.pallas_call(
        functools.partial(_pool_lru_kernel, ts=ts, sub=sub),
        grid=(b, s // rows),
        in_specs=[block(d)] + [_layer(a, l) for a in params],
        out_specs=[block(MIX), block(MIX)],
        out_shape=[jax.ShapeDtypeStruct((b, s, MIX), BF16)] * 2,
        scratch_shapes=[pltpu.VMEM((POOL_HALO + rows, MIX), F32), pltpu.VMEM((rows, MIX), BF16),
                        pltpu.VMEM((CONV_HALO + rows, 2 * MIX), F32), pltpu.VMEM((sub, ts, MIX), F32),
                        pltpu.VMEM((sub, ts, MIX), F32), pltpu.VMEM((rows, MIX), BF16), pltpu.VMEM((1, MIX), F32)],
        compiler_params=_params("arbitrary", "arbitrary"),
        name="pool_rglru_mixer",
    )(h, *params)


def _merge_kernel(x_ref, h_ref, wg_ref, ya_ref, yb_ref, yc_ref, yd_ref, wb_ref, wo_ref, o_ref):
    h = h_ref[...]
    merged = None
    for n, y_ref in enumerate((ya_ref, yb_ref, yc_ref, yd_ref)):
        gate = _sigmoid(_dot(h, wg_ref[:, n * D_MODEL:(n + 1) * D_MODEL]))
        term = gate * _dot(y_ref[...], wb_ref[n])
        merged = term if merged is None else merged + term
    o_ref[...] = x_ref[...] + _dot(merged.astype(BF16), wo_ref[...])


def _merge(x2, h2, l, wg, ys, wb, wo):
    t, d = x2.shape
    tm = min(TOK_TILE, t)
    row = lambda w: pl.BlockSpec((tm, w), lambda i: (i, 0))
    return pl.pallas_call(
        _merge_kernel,
        grid=(t // tm,),
        in_specs=[row(d), row(d), _layer(wg, l), row(MIX), row(MIX), row(MIX), row(MIX),
                  _layer(wb, l), _layer(wo, l)],
        out_specs=row(d),
        out_shape=jax.ShapeDtypeStruct((t, d), F32),
        compiler_params=_params("arbitrary"),
        name="gated_merge",
    )(x2, h2, wg, *ys, wb, wo)


FF_CHUNK = 1024


def _mlp_ple_kernel(x_ref, g_ref, w1_ref, w2_ref, gp_ref, wpg_ref, p_ref, wple_ref, gn_ref, *out_refs, last):
    x = x_ref[...]
    h = _rmsnorm(x, g_ref[...]).astype(BF16)
    acc = x
    for c in range(D_FF // FF_CHUNK):
        hid = jnp.maximum(_dot(h, w1_ref[:, c * FF_CHUNK:(c + 1) * FF_CHUNK]), 0.0)
        acc = acc + _dot((hid * hid).astype(BF16), w2_ref[c * FF_CHUNK:(c + 1) * FF_CHUNK, :])
    gate = _sigmoid(_dot(_rmsnorm(acc, gp_ref[...]).astype(BF16), wpg_ref[...]))
    y = acc + _dot(p_ref[0].astype(BF16), wple_ref[...]) * gate
    normed = _rmsnorm(y, gn_ref[...])
    if last:
        out_refs[0][...] = normed
    else:
        out_refs[0][...] = y
        out_refs[1][...] = normed.astype(BF16)


def _mlp_ple(x2, l, g, w1, w2, gp, wpg, p3, wple, g_next, last):
    t, d = x2.shape
    tm = min(TOK_TILE, t)
    row = lambda w: pl.BlockSpec((tm, w), lambda i: (i, 0))
    out_specs = [row(d)] if last else [row(d), row(d)]
    out_shape = [jax.ShapeDtypeStruct((t, d), F32)] + ([] if last else [jax.ShapeDtypeStruct((t, d), BF16)])
    return pl.pallas_call(
        functools.partial(_mlp_ple_kernel, last=last),
        grid=(t // tm,),
        in_specs=[row(d), _layer(g, l), _layer(w1, l, True), _layer(w2, l, True), _layer(gp, l),
                  _layer(wpg, l, True), pl.BlockSpec((1, tm, PLE_DIM), lambda i: (l, i, 0)),
                  _layer(wple, l, True), _layer(g_next, 0)],
        out_specs=out_specs,
        out_shape=out_shape,
        compiler_params=_params("arbitrary"),
        name="mlp_ple",
    )(x2, g, w1, w2, gp, wpg, p3, wple, g_next)


_SPLIT = (Q_LORA, KV_LORA, QK_ROPE, MIX, MIX, SSD_XBC, SSD_HEADS, MIX, MIX, N_BRANCH * D_MODEL)


def _col_offsets():
    offs, acc = [], 0
    for sz in _SPLIT:
        offs.append(acc)
        acc += sz
    return offs


def kernel(x, p, positions, g_mix, w_in, q_norm, w_uq, kv_norm, w_ukv, w_pool, pool_scale,
           ssd_conv_w, ssd_conv_b, ssd_dt_bias, ssd_a_log, ssd_d, ssd_norm,
           lru_conv_w, lru_conv_b, lru_w_a, lru_b_a, lru_w_i, lru_b_i, lru_lambda,
           w_branch, w_out, g_mlp, w_ff1, w_ff2, g_ple, w_ple_gate, w_ple, g_final):
    b, s, d = x.shape
    depth = w_in.shape[0]
    t = b * s
    offs = _col_offsets()
    o_pool, o_z, o_xbc, o_dt, o_lg, o_lx, o_gate = offs[3], offs[4], offs[5], offs[6], offs[7], offs[8], offs[9]
    cosq, sinq = _rope_tables(positions)
    bf = lambda a: a.astype(BF16)
    vec = lambda a: a.reshape(depth, 1, -1)
    win_a, wq, wk, wv = _mla_weights(w_in, w_uq, w_ukv)
    win_b = bf(w_in[..., o_pool:o_pool + MIX])
    w_dt = jnp.pad(w_in[..., o_dt:o_dt + SSD_HEADS], ((0, 0), (0, 0), (0, LANES - SSD_HEADS)))
    win_c = bf(jnp.concatenate([w_in[..., o_z:o_z + MIX], w_in[..., o_xbc:o_xbc + SSD_XBC], w_dt], axis=-1))
    head_lanes = lambda a: vec(jnp.pad(a, ((0, 0), (0, LANES - SSD_HEADS))))
    win_d = bf(w_in[..., o_lg:o_lg + 2 * MIX])
    wai = bf(jnp.concatenate([_block_diag(lru_w_a), _block_diag(lru_w_i)], axis=-1))
    bai = vec(jnp.concatenate([lru_b_a, lru_b_i], axis=-1))
    wg, wb, wo = bf(w_in[..., o_gate:]), bf(w_branch), bf(w_out)
    w1, w2, wpg, wple = bf(w_ff1), bf(w_ff2), bf(w_ple_gate), bf(w_ple)
    gm, gmlp, gple = vec(g_mix), vec(g_mlp), vec(g_ple)
    qn, kvn, ps = vec(q_norm), vec(kv_norm), vec(pool_scale)
    scb, dtb, alog = vec(ssd_conv_b), head_lanes(ssd_dt_bias), head_lanes(ssd_a_log)
    dskip, sng = vec(jnp.repeat(ssd_d, SSD_HEADDIM, axis=-1)), vec(ssd_norm)
    lcb, lam = vec(lru_conv_b), vec(lru_lambda)
    wp = bf(w_pool)
    p3 = p.reshape(depth, t, PLE_DIM)
    x2 = x.reshape(t, d)
    h2 = None
    for l in range(depth):
        if h2 is None:
            q, k, v, h = _mla_proj(x, gm, l, win_a, qn, wq, kvn, wk, wv, cosq, sinq)
            h2 = h.reshape(t, d)
        else:
            h = h2.reshape(b, s, d)
            q, k, v = _mla_proj(h, None, l, win_a, qn, wq, kvn, wk, wv, cosq, sinq)
        y_a = _attention(q, k, v)
        y_c = _ssd(h, l, win_c, ssd_conv_w, scb, dtb, alog, dskip, sng)
        y_b, y_d = _pool_lru(h, l, win_b, wp, ps, win_d, lru_conv_w, lcb, wai, bai, lam)
        ys = [y.reshape(t, MIX) for y in (y_a, y_b, y_c, y_d)]
        x2 = _merge(x2, h2, l, wg, ys, wb, wo)
        last = l == depth - 1
        g_next = g_final.reshape(1, 1, d) if last else gm[l + 1:l + 2]
        outs = _mlp_ple(x2, l, gmlp, w1, w2, gple, wpg, p3, wple, g_next, last)
        if last:
            x2 = outs[0]
        else:
            x2, h2 = outs
    return x2.reshape(b, s, d)
```
